```python
import jax, jax.numpy as jnp
from jax import lax
import numpy as np

D_MODEL = 1024
BATCH = 2
SEQ = 8192
DEPTH = 2

CTX_LEN = 256
GRID_W = 64
Q_BLOCK = 128
ROPE_THETA = 10000.0
NORM_EPS = 1e-6
N_MIXERS = 2

GQA_HEADS = 16
GQA_KV_HEADS = 4
GQA_GROUP = GQA_HEADS // GQA_KV_HEADS
GQA_HEAD_DIM = 64
GQA_WIDTH = GQA_HEADS * GQA_HEAD_DIM
GQA_KV_WIDTH = GQA_KV_HEADS * GQA_HEAD_DIM
GQA_IN_COLS = 2 * GQA_KV_WIDTH + 2 * GQA_WIDTH
GQA_CTX_COLS = 2 * GQA_KV_WIDTH

MLA_HEADS = 16
MLA_NOPE = 64
MLA_ROPE = 32
MLA_V = 64
MLA_Q_LORA = 384
MLA_KV_LORA = 256
MLA_WIDTH = MLA_HEADS * MLA_V
MLA_CTX_COLS = MLA_KV_LORA + MLA_ROPE
MLA_IN_COLS = MLA_CTX_COLS + MLA_Q_LORA + MLA_WIDTH

kernel_name = "hybrid_gqa_mla_diffusion_block"


def rms_norm(x, w):
    xf = x.astype(jnp.float32)
    y = xf * lax.rsqrt(jnp.mean(xf * xf, axis=-1, keepdims=True) + NORM_EPS)
    return (y * w.astype(jnp.float32)).astype(x.dtype)


def adaln_params(cond, w_mod, b_mod):
    mod = jax.nn.silu(cond) @ w_mod + b_mod
    return jnp.split(mod, 3, axis=-1)


def axial_rope_tables(n_tokens, rot_dim, dtype):
    rows = n_tokens // GRID_W
    row = jnp.repeat(jnp.arange(rows, dtype=jnp.float32), GRID_W)
    col = jnp.tile(jnp.arange(GRID_W, dtype=jnp.float32), rows)
    axis_dim = rot_dim // 2
    inv_freq = jnp.power(ROPE_THETA, -jnp.arange(0, axis_dim, 2, dtype=jnp.float32) / axis_dim)
    ang_r = row[:, None] * inv_freq[None, :]
    ang_c = col[:, None] * inv_freq[None, :]
    ang = jnp.concatenate([ang_r, ang_r, ang_c, ang_c], axis=-1)
    return jnp.cos(ang).astype(dtype), jnp.sin(ang).astype(dtype)


def apply_axial_rope(x, cos, sin):
    shape = (1, cos.shape[0]) + (1,) * (x.ndim - 3) + (cos.shape[-1],)
    cos = cos.reshape(shape)
    sin = sin.reshape(shape)
    x1, x2, x3, x4 = jnp.split(x, 4, axis=-1)
    rotated = jnp.concatenate([-x2, x1, -x4, x3], axis=-1)
    return x * cos + rotated * sin


def sweep_query_blocks(fn, *q_arrays):
    B, S = q_arrays[0].shape[:2]
    nb = S // Q_BLOCK
    blocks = tuple(jnp.moveaxis(a.reshape((B, nb, Q_BLOCK) + a.shape[2:]), 1, 0) for a in q_arrays)
    out = lax.map(lambda blk: fn(*blk), blocks)
    out = jnp.moveaxis(out, 0, 1)
    return out.reshape((B, S) + out.shape[3:])


def gqa_attend(q, k, v):
    s = jnp.einsum('bqhgd,bkhd->bhgqk', q, k).astype(jnp.float32) * (GQA_HEAD_DIM ** -0.5)
    p = jax.nn.softmax(s, axis=-1).astype(v.dtype)
    return jnp.einsum('bhgqk,bkhd->bqhgd', p, v)


def mla_attend(q_nope, q_rope, k_nope, k_rope, v):
    s = (jnp.einsum('bqhd,bkhd->bhqk', q_nope, k_nope)
         + jnp.einsum('bqhr,bkr->bhqk', q_rope, k_rope)).astype(jnp.float32) * ((MLA_NOPE + MLA_ROPE) ** -0.5)
    p = jax.nn.softmax(s, axis=-1).astype(v.dtype)
    return jnp.einsum('bhqk,bkhd->bqhd', p, v)


def gqa_mixer(h, hc, p, cos, sin, with_ctx_out):
    B, S, _ = h.shape
    L = hc.shape[1]
    cuts = [GQA_KV_WIDTH, 2 * GQA_KV_WIDTH, 2 * GQA_KV_WIDTH + GQA_WIDTH]
    k, v, q, g = jnp.split(h @ p["w_in"], cuts, axis=-1)
    q = apply_axial_rope(rms_norm(q.reshape(B, S, GQA_KV_HEADS, GQA_GROUP, GQA_HEAD_DIM), p["q_norm"]), cos, sin)
    k = apply_axial_rope(rms_norm(k.reshape(B, S, GQA_KV_HEADS, GQA_HEAD_DIM), p["k_norm"]), cos, sin)
    v = v.reshape(B, S, GQA_KV_HEADS, GQA_HEAD_DIM)
    w_ctx = p["w_in"] if with_ctx_out else p["w_in"][:, :GQA_CTX_COLS]
    proj_c = hc @ w_ctx
    kc = rms_norm(proj_c[..., :GQA_KV_WIDTH].reshape(B, L, GQA_KV_HEADS, GQA_HEAD_DIM), p["k_norm"])
    vc = proj_c[..., GQA_KV_WIDTH:GQA_CTX_COLS].reshape(B, L, GQA_KV_HEADS, GQA_HEAD_DIM)
    k_all = jnp.concatenate([kc, k], axis=1)
    v_all = jnp.concatenate([vc, v], axis=1)
    o = sweep_query_blocks(lambda qb: gqa_attend(qb, k_all, v_all), q).reshape(B, S, GQA_WIDTH)
    y = (o * jax.nn.silu(g)) @ p["w_out"]
    yc = None
    if with_ctx_out:
        qc = rms_norm(proj_c[..., GQA_CTX_COLS:GQA_CTX_COLS + GQA_WIDTH]
                      .reshape(B, L, GQA_KV_HEADS, GQA_GROUP, GQA_HEAD_DIM), p["q_norm"])
        gc = proj_c[..., GQA_CTX_COLS + GQA_WIDTH:]
        oc = gqa_attend(qc, kc, vc).reshape(B, L, GQA_WIDTH)
        yc = (oc * jax.nn.silu(gc)) @ p["w_out"]
    return y, yc


def mla_mixer(h, hc, p, cos, sin, with_ctx_out):
    B, S, _ = h.shape
    L = hc.shape[1]

    def kv_heads(kv_a, k_r, n):
        kv = (rms_norm(kv_a, p["kv_a_norm"]) @ p["w_kv_b"]).reshape(B, n, MLA_HEADS, MLA_NOPE + MLA_V)
        k_nope, v = jnp.split(kv, [MLA_NOPE], axis=-1)
        return rms_norm(k_nope, p["k_nope_norm"]), rms_norm(k_r, p["k_rope_norm"]), v

    def q_heads(q_a, n):
        q = (rms_norm(q_a, p["q_a_norm"]) @ p["w_q_b"]).reshape(B, n, MLA_HEADS, MLA_NOPE + MLA_ROPE)
        q = rms_norm(q, p["q_norm"])
        return jnp.split(q, [MLA_NOPE], axis=-1)

    cuts = [MLA_KV_LORA, MLA_CTX_COLS, MLA_CTX_COLS + MLA_Q_LORA]
    kv_a, k_r, q_a, g = jnp.split(h @ p["w_in"], cuts, axis=-1)
    k_nope, k_rope, v = kv_heads(kv_a, k_r, S)
    k_rope = apply_axial_rope(k_rope, cos, sin)
    q_nope, q_rope = q_heads(q_a, S)
    q_rope = apply_axial_rope(q_rope, cos, sin)
    w_ctx = p["w_in"] if with_ctx_out else p["w_in"][:, :MLA_CTX_COLS]
    proj_c = hc @ w_ctx
    kc_nope, kc_rope, vc = kv_heads(proj_c[..., :MLA_KV_LORA], proj_c[..., MLA_KV_LORA:MLA_CTX_COLS], L)
    kn_all = jnp.concatenate([kc_nope, k_nope], axis=1)
    kr_all = jnp.concatenate([kc_rope, k_rope], axis=1)
    v_all = jnp.concatenate([vc, v], axis=1)
    o = sweep_query_blocks(lambda qn, qr: mla_attend(qn, qr, kn_all, kr_all, v_all), q_nope, q_rope)
    y = (o.reshape(B, S, MLA_WIDTH) * jax.nn.silu(g)) @ p["w_out"]
    yc = None
    if with_ctx_out:
        qc_nope, qc_rope = q_heads(proj_c[..., MLA_CTX_COLS:MLA_CTX_COLS + MLA_Q_LORA], L)
        gc = proj_c[..., MLA_CTX_COLS + MLA_Q_LORA:]
        oc = mla_attend(qc_nope, qc_rope, kc_nope, kc_rope, vc).reshape(B, L, MLA_WIDTH)
        yc = (oc * jax.nn.silu(gc)) @ p["w_out"]
    return y, yc


def hybrid_layer(x, ctx, c, c_ctx, w_mod, b_mod, norm_w, mixer, mixer_params, cos, sin, with_ctx_out):
    shift, scale, gate = adaln_params(c, w_mod, b_mod)
    shift_c, scale_c, gate_c = adaln_params(c_ctx, w_mod, b_mod)
    h = rms_norm(x, norm_w) * (1.0 + scale[:, None, :]) + shift[:, None, :]
    hc = rms_norm(ctx, norm_w) * (1.0 + scale_c) + shift_c
    y, yc = mixer(h, hc, mixer_params, cos, sin, with_ctx_out)
    x = x + gate[:, None, :] * y
    if with_ctx_out:
        ctx = ctx + gate_c * yc
    return x, ctx


def _normal(key, shape, scale):
    return jax.random.normal(key, shape, dtype=jnp.float32) * scale


def _gain(key, n):
    return 1.0 + 0.01 * jax.random.normal(key, (n,), dtype=jnp.float32)


def setup_inputs(seed: int = 0) -> dict:
    key = jax.random.key(seed)
    ks = jax.random.split(key, 24)
    D = D_MODEL
    return {
        "x": _normal(ks[0], (BATCH, SEQ, D), 1.0),
        "c": _normal(ks[1], (BATCH, D), 1.0),
        "ctx": _normal(ks[2], (BATCH, CTX_LEN, D), 1.0),
        "c_ctx": _normal(ks[3], (D,), 1.0),
        "l0_w_mod": _normal(ks[4], (D, 3 * D), 0.5 * D ** -0.5),
        "l0_b_mod": _normal(ks[5], (3 * D,), 0.01),
        "l0_norm": _gain(ks[6], D),
        "l0_w_in": _normal(ks[7], (D, GQA_IN_COLS), D ** -0.5),
        "l0_q_norm": _gain(ks[8], GQA_HEAD_DIM),
        "l0_k_norm": _gain(ks[9], GQA_HEAD_DIM),
        "l0_w_out": _normal(ks[10], (GQA_WIDTH, D), GQA_WIDTH ** -0.5),
        "l1_w_mod": _normal(ks[11], (D, 3 * D), 0.5 * D ** -0.5),
        "l1_b_mod": _normal(ks[12], (3 * D,), 0.01),
        "l1_norm": _gain(ks[13], D),
        "l1_w_in": _normal(ks[14], (D, MLA_IN_COLS), D ** -0.5),
        "l1_kv_a_norm": _gain(ks[15], MLA_KV_LORA),
        "l1_w_kv_b": _normal(ks[16], (MLA_KV_LORA, MLA_HEADS * (MLA_NOPE + MLA_V)), MLA_KV_LORA ** -0.5),
        "l1_q_a_norm": _gain(ks[17], MLA_Q_LORA),
        "l1_w_q_b": _normal(ks[18], (MLA_Q_LORA, MLA_HEADS * (MLA_NOPE + MLA_ROPE)), MLA_Q_LORA ** -0.5),
        "l1_q_norm": _gain(ks[19], MLA_NOPE + MLA_ROPE),
        "l1_k_nope_norm": _gain(ks[20], MLA_NOPE),
        "l1_k_rope_norm": _gain(ks[21], MLA_ROPE),
        "l1_w_out": _normal(ks[22], (MLA_WIDTH, D), MLA_WIDTH ** -0.5),
    }


def reference(x, c, ctx, c_ctx,
              l0_w_mod, l0_b_mod, l0_norm, l0_w_in, l0_q_norm, l0_k_norm, l0_w_out,
              l1_w_mod, l1_b_mod, l1_norm, l1_w_in, l1_kv_a_norm, l1_w_kv_b, l1_q_a_norm, l1_w_q_b,
              l1_q_norm, l1_k_nope_norm, l1_k_rope_norm, l1_w_out):
    n_tokens = x.shape[1]
    cos_a, sin_a = axial_rope_tables(n_tokens, GQA_HEAD_DIM, x.dtype)
    cos_b, sin_b = axial_rope_tables(n_tokens, MLA_ROPE, x.dtype)
    gqa_params = {"w_in": l0_w_in, "q_norm": l0_q_norm, "k_norm": l0_k_norm, "w_out": l0_w_out}
    mla_params = {"w_in": l1_w_in, "kv_a_norm": l1_kv_a_norm, "w_kv_b": l1_w_kv_b,
                  "q_a_norm": l1_q_a_norm, "w_q_b": l1_w_q_b, "q_norm": l1_q_norm,
                  "k_nope_norm": l1_k_nope_norm, "k_rope_norm": l1_k_rope_norm, "w_out": l1_w_out}
    layers = [
        (l0_w_mod, l0_b_mod, l0_norm, gqa_mixer, gqa_params, cos_a, sin_a),
        (l1_w_mod, l1_b_mod, l1_norm, mla_mixer, mla_params, cos_b, sin_b),
    ]
    for i in range(DEPTH):
        w_mod, b_mod, norm_w, mixer, params, cos, sin = layers[i]
        x, ctx = hybrid_layer(x, ctx, c, c_ctx, w_mod, b_mod, norm_w, mixer, params, cos, sin,
                              with_ctx_out=(i < DEPTH - 1))
    return x
```

```python
import functools
import math

import jax
import jax.numpy as jnp
from jax import lax
from jax.experimental import pallas as pl
from jax.experimental.pallas import tpu as pltpu

D_MODEL = 1024
GRID_W = 64
ROPE_THETA = 10000.0
NORM_EPS = 1e-6

GQA_HEADS = 16
GQA_KV_HEADS = 4
GQA_GROUP = GQA_HEADS // GQA_KV_HEADS
GQA_HEAD_DIM = 64
GQA_WIDTH = GQA_HEADS * GQA_HEAD_DIM
GQA_KV_WIDTH = GQA_KV_HEADS * GQA_HEAD_DIM

MLA_HEADS = 16
MLA_NOPE = 64
MLA_ROPE = 32
MLA_V = 64
MLA_Q_LORA = 384
MLA_KV_LORA = 256
MLA_QK = MLA_NOPE + MLA_ROPE
MLA_WIDTH = MLA_HEADS * MLA_V

LANES = 128
MXU_COLS = 256
BF16_SUBLANES = 16
VMEM_LIMIT_BYTES = 48 * 1024 * 1024

ROW_TILE = 256
KV_TILE = ROW_TILE
Q_CHUNK = MXU_COLS
HEAD_V = 64
V_ROWS = HEAD_V + BF16_SUBLANES
MLA_QK_PAD = LANES
LOG2E = math.log2(math.e)
M_INIT = -1e30

_BF16 = jnp.bfloat16
_F32 = jnp.float32


def _silu(v):
    return v * jax.nn.sigmoid(v)


def _compiler_params(n_grid_axes):
    return pltpu.CompilerParams(
        dimension_semantics=("arbitrary",) * n_grid_axes,
        vmem_limit_bytes=VMEM_LIMIT_BYTES,
    )


def _adaln_kernel(cond_ref, w_ref, b_ref, o_ref):
    a = _silu(cond_ref[...])
    o_ref[...] = jnp.dot(a, w_ref[...], preferred_element_type=_F32,
                         precision=lax.Precision.HIGHEST) + b_ref[...]


def _adaln(cond, w_mod, b_mod):
    rows, d = cond.shape
    n = w_mod.shape[1]
    tn = 1024
    return pl.pallas_call(
        _adaln_kernel,
        grid=(n // tn,),
        in_specs=[
            pl.BlockSpec((rows, d), lambda j: (0, 0)),
            pl.BlockSpec((d, tn), lambda j: (0, j)),
            pl.BlockSpec((1, tn), lambda j: (0, j)),
        ],
        out_specs=pl.BlockSpec((rows, tn), lambda j: (0, j)),
        out_shape=jax.ShapeDtypeStruct((rows, n), _F32),
        compiler_params=_compiler_params(1),
        name="adaln",
    )(cond, w_mod, b_mod.reshape(1, n))


def _modulated_norm(x, nw, scale, shift):
    ms = jnp.mean(x * x, axis=-1, keepdims=True)
    return (x * lax.rsqrt(ms + NORM_EPS) * nw) * (1.0 + scale) + shift


def _head_rms_norm_t(t, w):
    ms = jnp.mean(t * t, axis=1, keepdims=True)
    return t * lax.rsqrt(ms + NORM_EPS) * w


def _axial_rope_t(t, cos, sin):
    q = t.shape[1] // 4
    x1, x2, x3, x4 = t[:, 0:q], t[:, q:2 * q], t[:, 2 * q:3 * q], t[:, 3 * q:4 * q]
    rot = jnp.concatenate([-x2, x1, -x4, x3], axis=1)
    return t * cos[None] + rot * sin[None]


def _ones_row_block(heads, cols):
    row = lax.broadcasted_iota(jnp.int32, (heads, BF16_SUBLANES, cols), 1)
    return jnp.where(row == 0, 1.0, 0.0).astype(_BF16)


def _gqa_prep_kernel(x_ref, shift_ref, scale_ref, nw_ref, wqkv_ref, wg_ref, qkw_ref, cos_ref, sin_ref,
                     q_ref, k_ref, v_ref, g_ref):
    ts = x_ref.shape[1]
    h = _modulated_norm(x_ref[0], nw_ref[...], scale_ref[0], shift_ref[0])
    g = jnp.dot(h.astype(_BF16), wg_ref[...], preferred_element_type=_F32)
    g_ref[0] = _silu(g).astype(_BF16)
    ht = h.T.astype(_BF16)
    a = jnp.dot(wqkv_ref[...], ht, preferred_element_type=_F32)
    n_qk = GQA_HEADS + GQA_KV_HEADS
    qk = a[:n_qk * GQA_HEAD_DIM].reshape(n_qk, GQA_HEAD_DIM, ts)
    qk = _head_rms_norm_t(qk, qkw_ref[...])
    qk = _axial_rope_t(qk, cos_ref[...], sin_ref[...])
    q_ref[0] = qk[:GQA_HEADS].astype(_BF16)
    kn = qk[GQA_HEADS:].reshape(GQA_KV_WIDTH, ts).T
    for hh in range(GQA_KV_HEADS):
        k_ref[0, hh] = kn[:, hh * GQA_HEAD_DIM:(hh + 1) * GQA_HEAD_DIM].astype(_BF16)
    vt = a[n_qk * GQA_HEAD_DIM:].reshape(GQA_KV_HEADS, GQA_HEAD_DIM, ts)
    v_ref[0, :, 0, 0:HEAD_V, :] = vt.astype(_BF16)
    v_ref[0, :, 0, HEAD_V:V_ROWS, :] = _ones_row_block(GQA_KV_HEADS, ts)


def _mod_spec(arr):
    per_batch = arr.shape[0] > 1
    return pl.BlockSpec((1, 1, D_MODEL), lambda b, i: (b if per_batch else 0, 0, 0))


def _gqa_prep(xr, shift, scale, nw, wqkv_t, wg, qkw, cos_t, sin_t):
    b, r, d = xr.shape
    ts = ROW_TILE
    nt = r // ts
    return pl.pallas_call(
        _gqa_prep_kernel,
        grid=(b, nt),
        in_specs=[
            pl.BlockSpec((1, ts, d), lambda bb, i: (bb, i, 0)),
            _mod_spec(shift), _mod_spec(scale),
            pl.BlockSpec((1, d), lambda bb, i: (0, 0)),
            pl.BlockSpec(wqkv_t.shape, lambda bb, i: (0, 0)),
            pl.BlockSpec(wg.shape, lambda bb, i: (0, 0)),
            pl.BlockSpec(qkw.shape, lambda bb, i: (0, 0, 0)),
            pl.BlockSpec((GQA_HEAD_DIM, ts), lambda bb, i: (0, i)),
            pl.BlockSpec((GQA_HEAD_DIM, ts), lambda bb, i: (0, i)),
        ],
        out_specs=[
            pl.BlockSpec((1, GQA_HEADS, GQA_HEAD_DIM, ts), lambda bb, i: (bb, 0, 0, i)),
            pl.BlockSpec((1, GQA_KV_HEADS, ts, GQA_HEAD_DIM), lambda bb, i: (bb, 0, i, 0)),
            pl.BlockSpec((1, GQA_KV_HEADS, 1, V_ROWS, ts), lambda bb, i: (bb, 0, i, 0, 0)),
            pl.BlockSpec((1, ts, GQA_WIDTH), lambda bb, i: (bb, i, 0)),
        ],
        out_shape=[
            jax.ShapeDtypeStruct((b, GQA_HEADS, GQA_HEAD_DIM, r), _BF16),
            jax.ShapeDtypeStruct((b, GQA_KV_HEADS, r, GQA_HEAD_DIM), _BF16),
            jax.ShapeDtypeStruct((b, GQA_KV_HEADS, nt, V_ROWS, ts), _BF16),
            jax.ShapeDtypeStruct((b, r, GQA_WIDTH), _BF16),
        ],
        compiler_params=_compiler_params(2),
        name="gqa_prep",
    )(xr, shift, scale, nw, wqkv_t, wg, qkw, cos_t, sin_t)


def _mla_prep_kernel(x_ref, shift_ref, scale_ref, nw_ref, win_ref, kvaw_ref, qaw_ref, wkvb_ref, wqb_ref,
                     knw_ref, qw_ref, krw_ref, cos_ref, sin_ref, q_ref, k_ref, v_ref, g_ref):
    ts = x_ref.shape[1]
    h = _modulated_norm(x_ref[0], nw_ref[...], scale_ref[0], shift_ref[0])
    p = jnp.dot(h.astype(_BF16), win_ref[...], preferred_element_type=_F32)
    c0, c1, c2 = MLA_KV_LORA, MLA_KV_LORA + MLA_Q_LORA, MLA_KV_LORA + MLA_Q_LORA + MLA_WIDTH
    kv_a, q_a, g, kr = p[:, :c0], p[:, c0:c1], p[:, c1:c2], p[:, c2:]
    g_ref[0] = _silu(g).astype(_BF16)

    def row_norm(t, w):
        return t * lax.rsqrt(jnp.mean(t * t, axis=-1, keepdims=True) + NORM_EPS) * w

    kv_an = row_norm(kv_a, kvaw_ref[...])
    q_an = row_norm(q_a, qaw_ref[...])
    kvt = jnp.dot(wkvb_ref[...], kv_an.T.astype(_BF16), preferred_element_type=_F32)
    kn = _head_rms_norm_t(kvt[:MLA_HEADS * MLA_NOPE].reshape(MLA_HEADS, MLA_NOPE, ts), knw_ref[...])
    vt = kvt[MLA_HEADS * MLA_NOPE:].reshape(MLA_HEADS, MLA_V, ts)
    v_ref[0, :, 0, 0:HEAD_V, :] = vt.astype(_BF16)
    v_ref[0, :, 0, HEAD_V:V_ROWS, :] = _ones_row_block(MLA_HEADS, ts)

    qt = jnp.dot(wqb_ref[...], q_an.T.astype(_BF16), preferred_element_type=_F32)
    qt = _head_rms_norm_t(qt.reshape(MLA_HEADS, MLA_QK, ts), qw_ref[...])
    q_rope = _axial_rope_t(qt[:, MLA_NOPE:], cos_ref[...], sin_ref[...])
    pad = jnp.zeros((MLA_HEADS, MLA_QK_PAD - MLA_QK, ts), _F32)
    q_ref[0] = jnp.concatenate([qt[:, :MLA_NOPE], q_rope, pad], axis=1).astype(_BF16)

    krt = kr.T[:MLA_ROPE].reshape(1, MLA_ROPE, ts)
    krt = _axial_rope_t(_head_rms_norm_t(krt, krw_ref[...]), cos_ref[...], sin_ref[...])
    kft = jnp.concatenate([kn, jnp.broadcast_to(krt, (MLA_HEADS, MLA_ROPE, ts)), pad], axis=1)
    kf = kft.reshape(MLA_HEADS * MLA_QK_PAD, ts).T
    for hh in range(MLA_HEADS):
        k_ref[0, hh] = kf[:, hh * MLA_QK_PAD:(hh + 1) * MLA_QK_PAD].astype(_BF16)


def _mla_prep(xr, shift, scale, nw, win, kvaw, qaw, wkvb_t, wqb_t, knw, qw, krw, cos_t, sin_t):
    b, r, d = xr.shape
    ts = ROW_TILE
    nt = r // ts
    const2 = lambda bb, i: (0, 0)
    const3 = lambda bb, i: (0, 0, 0)
    return pl.pallas_call(
        _mla_prep_kernel,
        grid=(b, nt),
        in_specs=[
            pl.BlockSpec((1, ts, d), lambda bb, i: (bb, i, 0)),
            _mod_spec(shift), _mod_spec(scale),
            pl.BlockSpec((1, d), const2),
            pl.BlockSpec(win.shape, const2),
            pl.BlockSpec(kvaw.shape, const2),
            pl.BlockSpec(qaw.shape, const2),
            pl.BlockSpec(wkvb_t.shape, const2),
            pl.BlockSpec(wqb_t.shape, const2),
            pl.BlockSpec(knw.shape, const3),
            pl.BlockSpec(qw.shape, const3),
            pl.BlockSpec(krw.shape, const3),
            pl.BlockSpec((MLA_ROPE, ts), lambda bb, i: (0, i)),
            pl.BlockSpec((MLA_ROPE, ts), lambda bb, i: (0, i)),
        ],
        out_specs=[
            pl.BlockSpec((1, MLA_HEADS, MLA_QK_PAD, ts), lambda bb, i: (bb, 0, 0, i)),
            pl.BlockSpec((1, MLA_HEADS, ts, MLA_QK_PAD), lambda bb, i: (bb, 0, i, 0)),
            pl.BlockSpec((1, MLA_HEADS, 1, V_ROWS, ts), lambda bb, i: (bb, 0, i, 0, 0)),
            pl.BlockSpec((1, ts, MLA_WIDTH), lambda bb, i: (bb, i, 0)),
        ],
        out_shape=[
            jax.ShapeDtypeStruct((b, MLA_HEADS, MLA_QK_PAD, r), _BF16),
            jax.ShapeDtypeStruct((b, MLA_HEADS, r, MLA_QK_PAD), _BF16),
            jax.ShapeDtypeStruct((b, MLA_HEADS, nt, V_ROWS, ts), _BF16),
            jax.ShapeDtypeStruct((b, r, MLA_WIDTH), _BF16),
        ],
        compiler_params=_compiler_params(2),
        name="mla_prep",
    )(xr, shift, scale, nw, win, kvaw, qaw, wkvb_t, wqb_t, knw, qw, krw, cos_t, sin_t)


def _attn_kernel(*refs, heads, chunks_per_head, n_lat_tiles, n_ctx_tiles):
    if n_lat_tiles:
        q_ref, kl_ref, vl_ref, kc_ref, vc_ref, o_ref, m_ref, acc_ref = refs
    else:
        q_ref, kc_ref, vc_ref, o_ref, m_ref, acc_ref = refs
    n_chunks = heads * chunks_per_head
    m_ref[...] = jnp.full(m_ref.shape, M_INIT, _F32)
    acc_ref[...] = jnp.zeros(acc_ref.shape, _F32)

    def q_chunk(c):
        hh, cc = divmod(c, chunks_per_head)
        return hh, slice(cc * Q_CHUNK, (cc + 1) * Q_CHUNK)

    def tile_update(kt, vt):
        for c in range(n_chunks):
            hh, cols = q_chunk(c)
            s = jnp.dot(kt, q_ref[0, hh, :, cols], preferred_element_type=_F32)
            m_prev = m_ref[c]
            m_new = jnp.maximum(m_prev, jnp.max(s, axis=0, keepdims=True))
            alpha = jnp.exp2(m_prev - m_new)
            p = jnp.exp2(s - m_new).astype(_BF16)
            pv = jnp.dot(vt, p, preferred_element_type=_F32)
            acc_ref[c] = alpha * acc_ref[c] + pv
            m_ref[c] = m_new

    if n_lat_tiles:
        def lat_body(j, carry):
            start = pl.multiple_of(j * KV_TILE, KV_TILE)
            tile_update(kl_ref[0, 0, pl.ds(start, KV_TILE), :], vl_ref[0, 0, j])
            return carry
        lax.fori_loop(0, n_lat_tiles, lat_body, 0)
    for j in range(n_ctx_tiles):
        tile_update(kc_ref[0, 0, j * KV_TILE:(j + 1) * KV_TILE, :], vc_ref[0, 0, j])

    for c in range(n_chunks):
        hh, cols = q_chunk(c)
        acc = acc_ref[c]
        o_ref[0, hh, :, cols] = (acc[:HEAD_V] / acc[HEAD_V:HEAD_V + 1]).astype(_BF16)


def _attention(qt, k_lat, v_lat, k_ctx, v_ctx, *, heads_per_step, tq, name):
    b, h, dk, r = qt.shape
    kvh = k_ctx.shape[1]
    n_ctx_tiles = v_ctx.shape[2]
    n_lat_tiles = v_lat.shape[2] if v_lat is not None else 0
    chunks_per_head = tq // Q_CHUNK
    n_chunks = heads_per_step * chunks_per_head

    def kv_specs(k, v):
        return [pl.BlockSpec((1, 1) + k.shape[2:], lambda bb, g, i: (bb, g, 0, 0)),
                pl.BlockSpec((1, 1) + v.shape[2:], lambda bb, g, i: (bb, g, 0, 0, 0))]

    in_specs = [pl.BlockSpec((1, heads_per_step, dk, tq), lambda bb, g, i: (bb, g, 0, i))]
    args = [qt]
    if n_lat_tiles:
        in_specs += kv_specs(k_lat, v_lat)
        args += [k_lat, v_lat]
    in_specs += kv_specs(k_ctx, v_ctx)
    args += [k_ctx, v_ctx]
    return pl.pallas_call(
        functools.partial(_attn_kernel, heads=heads_per_step, chunks_per_head=chunks_per_head,
                          n_lat_tiles=n_lat_tiles, n_ctx_tiles=n_ctx_tiles),
        grid=(b, kvh, r // tq),
        in_specs=in_specs,
        out_specs=pl.BlockSpec((1, heads_per_step, HEAD_V, tq), lambda bb, g, i: (bb, g, 0, i)),
        out_shape=jax.ShapeDtypeStruct((b, h, HEAD_V, r), _BF16),
        scratch_shapes=[pltpu.VMEM((n_chunks, 1, Q_CHUNK), _F32),
                        pltpu.VMEM((n_chunks, V_ROWS, Q_CHUNK), _F32)],
        compiler_params=_compiler_params(3),
        name=name,
    )(*args)


def _outproj_kernel(o_ref, g_ref, x_ref, gate_ref, w_ref, y_ref):
    ts = x_ref.shape[1]
    o = o_ref[0].astype(_F32).reshape(o_ref.shape[1] * o_ref.shape[2], ts).T
    z = (o * g_ref[0].astype(_F32)).astype(_BF16)
    y = jnp.dot(z, w_ref[...], preferred_element_type=_F32)
    y_ref[0] = x_ref[0] + gate_ref[0] * y


def _outproj(ot, gs, xr, gate, w_out):
    b, r, d = xr.shape
    ts = ROW_TILE
    heads, hv = ot.shape[1], ot.shape[2]
    return pl.pallas_call(
        _outproj_kernel,
        grid=(b, r // ts),
        in_specs=[
            pl.BlockSpec((1, heads, hv, ts), lambda bb, i: (bb, 0, 0, i)),
            pl.BlockSpec((1, ts, heads * hv), lambda bb, i: (bb, i, 0)),
            pl.BlockSpec((1, ts, d), lambda bb, i: (bb, i, 0)),
            _mod_spec(gate),
            pl.BlockSpec(w_out.shape, lambda bb, i: (0, 0)),
        ],
        out_specs=pl.BlockSpec((1, ts, d), lambda bb, i: (bb, i, 0)),
        out_shape=jax.ShapeDtypeStruct((b, r, d), _F32),
        compiler_params=_compiler_params(2),
        name="outproj",
    )(ot, gs, xr, gate, w_out)


def _axial_rope_tables_t(n_tokens, rot_dim):
    rows = n_tokens // GRID_W
    row = jnp.repeat(jnp.arange(rows, dtype=_F32), GRID_W)
    col = jnp.tile(jnp.arange(GRID_W, dtype=_F32), rows)
    axis_dim = rot_dim // 2
    inv_freq = jnp.power(ROPE_THETA, -jnp.arange(0, axis_dim, 2, dtype=_F32) / axis_dim)
    ang_r = row[:, None] * inv_freq[None, :]
    ang_c = col[:, None] * inv_freq[None, :]
    ang = jnp.concatenate([ang_r, ang_r, ang_c, ang_c], axis=-1)
    return jnp.cos(ang).T, jnp.sin(ang).T


def _identity_rope_tables_t(n_tokens, rot_dim):
    return jnp.ones((rot_dim, n_tokens), _F32), jnp.zeros((rot_dim, n_tokens), _F32)


def _split_mod(mod, n_batch):
    d = D_MODEL
    lat = tuple(mod[:n_batch, None, k * d:(k + 1) * d] for k in range(3))
    ctx = tuple(mod[n_batch:n_batch + 1, None, k * d:(k + 1) * d] for k in range(3))
    return lat, ctx


def kernel(x, c, ctx, c_ctx, l0_w_mod, l0_b_mod, l0_norm, l0_w_in, l0_q_norm, l0_k_norm, l0_w_out, l1_w_mod, l1_b_mod, l1_norm, l1_w_in, l1_kv_a_norm, l1_w_kv_b, l1_q_a_norm, l1_w_q_b, l1_q_norm, l1_k_nope_norm, l1_k_rope_norm, l1_w_out):
    n_batch, seq, d = x.shape
    ctx_len = ctx.shape[1]
    cond_rows = 8
    cond = jnp.concatenate([c, c_ctx[None, :], jnp.zeros((cond_rows - n_batch - 1, d), _F32)], axis=0)

    (sh_l, sc_l, gt_l), (sh_c, sc_c, gt_c) = _split_mod(_adaln(cond, l0_w_mod, l0_b_mod), n_batch)
    kw, kv_w = GQA_KV_WIDTH, 2 * GQA_KV_WIDTH
    w_k, w_v = l0_w_in[:, :kw], l0_w_in[:, kw:kv_w]
    w_q, w_g = l0_w_in[:, kv_w:kv_w + GQA_WIDTH], l0_w_in[:, kv_w + GQA_WIDTH:]
    wqkv_t = jnp.concatenate([w_q, w_k, w_v], axis=1).T.astype(_BF16)
    q_scale = GQA_HEAD_DIM ** -0.5 * LOG2E
    qkw = jnp.concatenate([jnp.tile((l0_q_norm * q_scale)[None], (GQA_HEADS, 1)),
                           jnp.tile(l0_k_norm[None], (GQA_KV_HEADS, 1))], axis=0)[:, :, None]
    nw0 = l0_norm[None, :]
    wg0 = w_g.astype(_BF16)
    cos_a, sin_a = _axial_rope_tables_t(seq, GQA_HEAD_DIM)
    cos_i, sin_i = _identity_rope_tables_t(ctx_len, GQA_HEAD_DIM)
    q_l, k_l, v_l, g_l = _gqa_prep(x, sh_l, sc_l, nw0, wqkv_t, wg0, qkw, cos_a, sin_a)
    q_c, k_c, v_c, g_c = _gqa_prep(ctx, sh_c, sc_c, nw0, wqkv_t, wg0, qkw, cos_i, sin_i)
    o_l = _attention(q_l, k_l, v_l, k_c, v_c, heads_per_step=GQA_GROUP, tq=512, name="gqa_attn")
    o_c = _attention(q_c, None, None, k_c, v_c, heads_per_step=GQA_GROUP, tq=ctx_len, name="gqa_attn_ctx")
    w_out0 = l0_w_out.astype(_BF16)
    x = _outproj(o_l, g_l, x, gt_l, w_out0)
    ctx = _outproj(o_c, g_c, ctx, gt_c, w_out0)

    (sh_l, sc_l, gt_l), (sh_c, sc_c, _) = _split_mod(_adaln(cond, l1_w_mod, l1_b_mod), n_batch)
    c_kv, c_kr = MLA_KV_LORA, MLA_KV_LORA + MLA_ROPE
    c_q = c_kr + MLA_Q_LORA
    win = jnp.concatenate([l1_w_in[:, :c_kv], l1_w_in[:, c_kr:c_q], l1_w_in[:, c_q:], l1_w_in[:, c_kv:c_kr],
                           jnp.zeros((d, LANES - MLA_ROPE), _F32)], axis=1).astype(_BF16)
    wkvb = l1_w_kv_b.reshape(MLA_KV_LORA, MLA_HEADS, MLA_NOPE + MLA_V)
    wkvb_t = jnp.concatenate([wkvb[:, :, :MLA_NOPE].reshape(MLA_KV_LORA, -1),
                              wkvb[:, :, MLA_NOPE:].reshape(MLA_KV_LORA, -1)], axis=1).T.astype(_BF16)
    wqb_t = l1_w_q_b.T.astype(_BF16)
    q_scale = MLA_QK ** -0.5 * LOG2E
    nw1 = l1_norm[None, :]
    kvaw, qaw = l1_kv_a_norm[None, :], l1_q_a_norm[None, :]
    knw = l1_k_nope_norm[None, :, None]
    qw = (l1_q_norm * q_scale)[None, :, None]
    krw = l1_k_rope_norm[None, :, None]
    cos_b, sin_b = _axial_rope_tables_t(seq, MLA_ROPE)
    cos_i, sin_i = _identity_rope_tables_t(ctx_len, MLA_ROPE)
    q_l, k_l, v_l, g_l = _mla_prep(x, sh_l, sc_l, nw1, win, kvaw, qaw, wkvb_t, wqb_t, knw, qw, krw, cos_b, sin_b)
    _, k_c, v_c, _ = _mla_prep(ctx, sh_c, sc_c, nw1, win, kvaw, qaw, wkvb_t, wqb_t, knw, qw, krw, cos_i, sin_i)
    o_l = _attention(q_l, k_l, v_l, k_c, v_c, heads_per_step=1, tq=2048, name="mla_attn")
    return _outproj(o_l, g_l, x, gt_l, l1_w_out.astype(_BF16))
```

```python
import functools
import math

import jax
import jax.numpy as jnp
from jax import lax
from jax.experimental import pallas as pl
from jax.experimental.pallas import tpu as pltpu

D_MODEL = 1024
GRID_W = 64
ROPE_THETA = 10000.0
NORM_EPS = 1e-6

GQA_HEADS = 16
GQA_KV_HEADS = 4
GQA_GROUP = GQA_HEADS // GQA_KV_HEADS
GQA_HEAD_DIM = 64
GQA_WIDTH = GQA_HEADS * GQA_HEAD_DIM
GQA_KV_WIDTH = GQA_KV_HEADS * GQA_HEAD_DIM

MLA_HEADS = 16
MLA_NOPE = 64
MLA_ROPE = 32
MLA_V = 64
MLA_Q_LORA = 384
MLA_KV_LORA = 256
MLA_QK = MLA_NOPE + MLA_ROPE
MLA_WIDTH = MLA_HEADS * MLA_V

LANES = 128
MXU_COLS = 256
BF16_SUBLANES = 16
VMEM_LIMIT_BYTES = 48 * 1024 * 1024

ROW_TILE = 256
KV_TILE = ROW_TILE
Q_CHUNK = MXU_COLS
HEAD_V = 64
V_ROWS = HEAD_V + BF16_SUBLANES
MLA_QK_PAD = LANES
QK_AHEAD = 4
TILES_PER_ITER = 4
LOG2E = math.log2(math.e)
M_INIT = -1e30

_BF16 = jnp.bfloat16
_F32 = jnp.float32


def _silu(v):
    return v * jax.nn.sigmoid(v)


def _compiler_params(n_grid_axes):
    return pltpu.CompilerParams(
        dimension_semantics=("arbitrary",) * n_grid_axes,
        vmem_limit_bytes=VMEM_LIMIT_BYTES,
    )


def _adaln_kernel(cond_ref, w_ref, b_ref, o_ref):
    a = _silu(cond_ref[...])
    o_ref[...] = jnp.dot(a, w_ref[...], preferred_element_type=_F32,
                         precision=lax.Precision.HIGHEST) + b_ref[...]


def _adaln(cond, w_mod, b_mod):
    rows, d = cond.shape
    n = w_mod.shape[1]
    tn = 1024
    return pl.pallas_call(
        _adaln_kernel,
        grid=(n // tn,),
        in_specs=[
            pl.BlockSpec((rows, d), lambda j: (0, 0)),
            pl.BlockSpec((d, tn), lambda j: (0, j)),
            pl.BlockSpec((1, tn), lambda j: (0, j)),
        ],
        out_specs=pl.BlockSpec((rows, tn), lambda j: (0, j)),
        out_shape=jax.ShapeDtypeStruct((rows, n), _F32),
        compiler_params=_compiler_params(1),
        name="adaln",
    )(cond, w_mod, b_mod.reshape(1, n))


def _modulated_norm(x, nw, scale, shift):
    ms = jnp.mean(x * x, axis=-1, keepdims=True)
    return (x * lax.rsqrt(ms + NORM_EPS) * nw) * (1.0 + scale) + shift


def _head_rms_norm_t(t, w):
    ms = jnp.mean(t * t, axis=1, keepdims=True)
    return t * lax.rsqrt(ms + NORM_EPS) * w


def _axial_rope_t(t, cos, sin):
    q = t.shape[1] // 4
    x1, x2, x3, x4 = t[:, 0:q], t[:, q:2 * q], t[:, 2 * q:3 * q], t[:, 3 * q:4 * q]
    rot = jnp.concatenate([-x2, x1, -x4, x3], axis=1)
    return t * cos[None] + rot * sin[None]


def _ones_row_block(heads, cols):
    row = lax.broadcasted_iota(jnp.int32, (heads, BF16_SUBLANES, cols), 1)
    return jnp.where(row == 0, 1.0, 0.0).astype(_BF16)


def _gqa_prep_kernel(x_ref, shift_ref, scale_ref, nw_ref, wqkv_ref, wg_ref, qkw_ref, cos_ref, sin_ref,
                     q_ref, k_ref, v_ref, g_ref):
    ts = x_ref.shape[1]
    h = _modulated_norm(x_ref[0], nw_ref[...], scale_ref[0], shift_ref[0])
    g = jnp.dot(h.astype(_BF16), wg_ref[...], preferred_element_type=_F32)
    g_ref[0] = _silu(g).astype(_BF16)
    ht = h.T.astype(_BF16)
    a = jnp.dot(wqkv_ref[...], ht, preferred_element_type=_F32)
    n_qk = GQA_HEADS + GQA_KV_HEADS
    qk = a[:n_qk * GQA_HEAD_DIM].reshape(n_qk, GQA_HEAD_DIM, ts)
    qk = _head_rms_norm_t(qk, qkw_ref[...])
    qk = _axial_rope_t(qk, cos_ref[...], sin_ref[...])
    q_ref[0] = qk[:GQA_HEADS].astype(_BF16)
    kn = qk[GQA_HEADS:].reshape(GQA_KV_WIDTH, ts).T
    for hh in range(GQA_KV_HEADS):
        k_ref[0, hh] = kn[:, hh * GQA_HEAD_DIM:(hh + 1) * GQA_HEAD_DIM].astype(_BF16)
    vt = a[n_qk * GQA_HEAD_DIM:].reshape(GQA_KV_HEADS, GQA_HEAD_DIM, ts)
    v_ref[0, :, 0, 0:HEAD_V, :] = vt.astype(_BF16)
    v_ref[0, :, 0, HEAD_V:V_ROWS, :] = _ones_row_block(GQA_KV_HEADS, ts)


def _mod_spec(arr):
    per_batch = arr.shape[0] > 1
    return pl.BlockSpec((1, 1, D_MODEL), lambda b, i: (b if per_batch else 0, 0, 0))


def _gqa_prep(xr, shift, scale, nw, wqkv_t, wg, qkw, cos_t, sin_t):
    b, r, d = xr.shape
    ts = ROW_TILE
    nt = r // ts
    return pl.pallas_call(
        _gqa_prep_kernel,
        grid=(b, nt),
        in_specs=[
            pl.BlockSpec((1, ts, d), lambda bb, i: (bb, i, 0)),
            _mod_spec(shift), _mod_spec(scale),
            pl.BlockSpec((1, d), lambda bb, i: (0, 0)),
            pl.BlockSpec(wqkv_t.shape, lambda bb, i: (0, 0)),
            pl.BlockSpec(wg.shape, lambda bb, i: (0, 0)),
            pl.BlockSpec(qkw.shape, lambda bb, i: (0, 0, 0)),
            pl.BlockSpec((GQA_HEAD_DIM, ts), lambda bb, i: (0, i)),
            pl.BlockSpec((GQA_HEAD_DIM, ts), lambda bb, i: (0, i)),
        ],
        out_specs=[
            pl.BlockSpec((1, GQA_HEADS, GQA_HEAD_DIM, ts), lambda bb, i: (bb, 0, 0, i)),
            pl.BlockSpec((1, GQA_KV_HEADS, ts, GQA_HEAD_DIM), lambda bb, i: (bb, 0, i, 0)),
            pl.BlockSpec((1, GQA_KV_HEADS, 1, V_ROWS, ts), lambda bb, i: (bb, 0, i, 0, 0)),
            pl.BlockSpec((1, ts, GQA_WIDTH), lambda bb, i: (bb, i, 0)),
        ],
        out_shape=[
            jax.ShapeDtypeStruct((b, GQA_HEADS, GQA_HEAD_DIM, r), _BF16),
            jax.ShapeDtypeStruct((b, GQA_KV_HEADS, r, GQA_HEAD_DIM), _BF16),
            jax.ShapeDtypeStruct((b, GQA_KV_HEADS, nt, V_ROWS, ts), _BF16),
            jax.ShapeDtypeStruct((b, r, GQA_WIDTH), _BF16),
        ],
        compiler_params=_compiler_params(2),
        name="gqa_prep",
    )(xr, shift, scale, nw, wqkv_t, wg, qkw, cos_t, sin_t)


def _mla_prep_kernel(x_ref, shift_ref, scale_ref, nw_ref, win_ref, kvaw_ref, qaw_ref, wkvb_ref, wqb_ref,
                     knw_ref, qw_ref, krw_ref, cos_ref, sin_ref, q_ref, k_ref, v_ref, g_ref):
    ts = x_ref.shape[1]
    h = _modulated_norm(x_ref[0], nw_ref[...], scale_ref[0], shift_ref[0])
    p = jnp.dot(h.astype(_BF16), win_ref[...], preferred_element_type=_F32)
    c0, c1, c2 = MLA_KV_LORA, MLA_KV_LORA + MLA_Q_LORA, MLA_KV_LORA + MLA_Q_LORA + MLA_WIDTH
    kv_a, q_a, g, kr = p[:, :c0], p[:, c0:c1], p[:, c1:c2], p[:, c2:]
    g_ref[0] = _silu(g).astype(_BF16)

    def row_norm(t, w):
        return t * lax.rsqrt(jnp.mean(t * t, axis=-1, keepdims=True) + NORM_EPS) * w

    kv_an = row_norm(kv_a, kvaw_ref[...])
    q_an = row_norm(q_a, qaw_ref[...])
    kvt = jnp.dot(wkvb_ref[...], kv_an.T.astype(_BF16), preferred_element_type=_F32)
    kn = _head_rms_norm_t(kvt[:MLA_HEADS * MLA_NOPE].reshape(MLA_HEADS, MLA_NOPE, ts), knw_ref[...])
    vt = kvt[MLA_HEADS * MLA_NOPE:].reshape(MLA_HEADS, MLA_V, ts)
    v_ref[0, :, 0, 0:HEAD_V, :] = vt.astype(_BF16)
    v_ref[0, :, 0, HEAD_V:V_ROWS, :] = _ones_row_block(MLA_HEADS, ts)

    qt = jnp.dot(wqb_ref[...], q_an.T.astype(_BF16), preferred_element_type=_F32)
    qt = _head_rms_norm_t(qt.reshape(MLA_HEADS, MLA_QK, ts), qw_ref[...])
    q_rope = _axial_rope_t(qt[:, MLA_NOPE:], cos_ref[...], sin_ref[...])
    pad = jnp.zeros((MLA_HEADS, MLA_QK_PAD - MLA_QK, ts), _F32)
    q_ref[0] = jnp.concatenate([qt[:, :MLA_NOPE], q_rope, pad], axis=1).astype(_BF16)

    krt = kr.T[:MLA_ROPE].reshape(1, MLA_ROPE, ts)
    krt = _axial_rope_t(_head_rms_norm_t(krt, krw_ref[...]), cos_ref[...], sin_ref[...])
    kft = jnp.concatenate([kn, jnp.broadcast_to(krt, (MLA_HEADS, MLA_ROPE, ts)), pad], axis=1)
    kf = kft.reshape(MLA_HEADS * MLA_QK_PAD, ts).T
    for hh in range(MLA_HEADS):
        k_ref[0, hh] = kf[:, hh * MLA_QK_PAD:(hh + 1) * MLA_QK_PAD].astype(_BF16)


def _mla_prep(xr, shift, scale, nw, win, kvaw, qaw, wkvb_t, wqb_t, knw, qw, krw, cos_t, sin_t):
    b, r, d = xr.shape
    ts = ROW_TILE
    nt = r // ts
    const2 = lambda bb, i: (0, 0)
    const3 = lambda bb, i: (0, 0, 0)
    return pl.pallas_call(
        _mla_prep_kernel,
        grid=(b, nt),
        in_specs=[
            pl.BlockSpec((1, ts, d), lambda bb, i: (bb, i, 0)),
            _mod_spec(shift), _mod_spec(scale),
            pl.BlockSpec((1, d), const2),
            pl.BlockSpec(win.shape, const2),
            pl.BlockSpec(kvaw.shape, const2),
            pl.BlockSpec(qaw.shape, const2),
            pl.BlockSpec(wkvb_t.shape, const2),
            pl.BlockSpec(wqb_t.shape, const2),
            pl.BlockSpec(knw.shape, const3),
            pl.BlockSpec(qw.shape, const3),
            pl.BlockSpec(krw.shape, const3),
            pl.BlockSpec((MLA_ROPE, ts), lambda bb, i: (0, i)),
            pl.BlockSpec((MLA_ROPE, ts), lambda bb, i: (0, i)),
        ],
        out_specs=[
            pl.BlockSpec((1, MLA_HEADS, MLA_QK_PAD, ts), lambda bb, i: (bb, 0, 0, i)),
            pl.BlockSpec((1, MLA_HEADS, ts, MLA_QK_PAD), lambda bb, i: (bb, 0, i, 0)),
            pl.BlockSpec((1, MLA_HEADS, 1, V_ROWS, ts), lambda bb, i: (bb, 0, i, 0, 0)),
            pl.BlockSpec((1, ts, MLA_WIDTH), lambda bb, i: (bb, i, 0)),
        ],
        out_shape=[
            jax.ShapeDtypeStruct((b, MLA_HEADS, MLA_QK_PAD, r), _BF16),
            jax.ShapeDtypeStruct((b, MLA_HEADS, r, MLA_QK_PAD), _BF16),
            jax.ShapeDtypeStruct((b, MLA_HEADS, nt, V_ROWS, ts), _BF16),
            jax.ShapeDtypeStruct((b, r, MLA_WIDTH), _BF16),
        ],
        compiler_params=_compiler_params(2),
        name="mla_prep",
    )(xr, shift, scale, nw, win, kvaw, qaw, wkvb_t, wqb_t, knw, qw, krw, cos_t, sin_t)


def _attn_kernel(*refs, heads, chunks_per_head, n_lat_tiles, n_ctx_tiles):
    if n_lat_tiles:
        q_ref, kl_ref, vl_ref, kc_ref, vc_ref, o_ref, m_ref, acc_ref = refs
    else:
        q_ref, kc_ref, vc_ref, o_ref, m_ref, acc_ref = refs
    n_chunks = heads * chunks_per_head
    m_ref[...] = jnp.full(m_ref.shape, M_INIT, _F32)
    acc_ref[...] = jnp.zeros(acc_ref.shape, _F32)

    def q_chunk(c):
        hh, cc = divmod(c, chunks_per_head)
        return hh, slice(cc * Q_CHUNK, (cc + 1) * Q_CHUNK)

    def tiles_update(k_ref, v_ref, tiles):
        steps = [(j, c) for j in tiles for c in range(n_chunks)]

        def scores(j, c):
            hh, cols = q_chunk(c)
            start = j * KV_TILE
            if not isinstance(j, int):
                start = pl.multiple_of(start, KV_TILE)
            kt = k_ref[0, 0, pl.ds(start, KV_TILE), :]
            return jnp.dot(kt, q_ref[0, hh, :, cols], preferred_element_type=_F32)

        pending = [scores(*st) for st in steps[:QK_AHEAD]]
        for t, (j, c) in enumerate(steps):
            s = pending.pop(0)
            if t + QK_AHEAD < len(steps):
                pending.append(scores(*steps[t + QK_AHEAD]))
            m_prev = m_ref[c]
            m_new = jnp.maximum(m_prev, jnp.max(s, axis=0, keepdims=True))
            alpha = jnp.exp2(m_prev - m_new)
            p = jnp.exp2(s - m_new).astype(_BF16)
            pv = jnp.dot(v_ref[0, 0, j], p, preferred_element_type=_F32)
            acc_ref[c] = alpha * acc_ref[c] + pv
            m_ref[c] = m_new

    if n_lat_tiles:
        def lat_body(jj, carry):
            tiles_update(kl_ref, vl_ref, [jj * TILES_PER_ITER + u for u in range(TILES_PER_ITER)])
            return carry
        lax.fori_loop(0, n_lat_tiles // TILES_PER_ITER, lat_body, 0)
    tiles_update(kc_ref, vc_ref, list(range(n_ctx_tiles)))

    for c in range(n_chunks):
        hh, cols = q_chunk(c)
        acc = acc_ref[c]
        o_ref[0, hh, :, cols] = (acc[:HEAD_V] / acc[HEAD_V:HEAD_V + 1]).astype(_BF16)


def _attention(qt, k_lat, v_lat, k_ctx, v_ctx, *, heads_per_step, tq, name):
    b, h, dk, r = qt.shape
    kvh = k_ctx.shape[1]
    n_ctx_tiles = v_ctx.shape[2]
    n_lat_tiles = v_lat.shape[2] if v_lat is not None else 0
    chunks_per_head = tq // Q_CHUNK
    n_chunks = heads_per_step * chunks_per_head

    def kv_specs(k, v):
        return [pl.BlockSpec((1, 1) + k.shape[2:], lambda bb, g, i: (bb, g, 0, 0)),
                pl.BlockSpec((1, 1) + v.shape[2:], lambda bb, g, i: (bb, g, 0, 0, 0))]

    in_specs = [pl.BlockSpec((1, heads_per_step, dk, tq), lambda bb, g, i: (bb, g, 0, i))]
    args = [qt]
    if n_lat_tiles:
        in_specs += kv_specs(k_lat, v_lat)
        args += [k_lat, v_lat]
    in_specs += kv_specs(k_ctx, v_ctx)
    args += [k_ctx, v_ctx]
    return pl.pallas_call(
        functools.partial(_attn_kernel, heads=heads_per_step, chunks_per_head=chunks_per_head,
                          n_lat_tiles=n_lat_tiles, n_ctx_tiles=n_ctx_tiles),
        grid=(b, kvh, r // tq),
        in_specs=in_specs,
        out_specs=pl.BlockSpec((1, heads_per_step, HEAD_V, tq), lambda bb, g, i: (bb, g, 0, i)),
        out_shape=jax.ShapeDtypeStruct((b, h, HEAD_V, r), _BF16),
        scratch_shapes=[pltpu.VMEM((n_chunks, 1, Q_CHUNK), _F32),
                        pltpu.VMEM((n_chunks, V_ROWS, Q_CHUNK), _F32)],
        compiler_params=_compiler_params(3),
        name=name,
    )(*args)


def _outproj_kernel(o_ref, g_ref, x_ref, gate_ref, w_ref, y_ref):
    ts = x_ref.shape[1]
    o = o_ref[0].astype(_F32).reshape(o_ref.shape[1] * o_ref.shape[2], ts).T
    z = (o * g_ref[0].astype(_F32)).astype(_BF16)
    y = jnp.dot(z, w_ref[...], preferred_element_type=_F32)
    y_ref[0] = x_ref[0] + gate_ref[0] * y


def _outproj(ot, gs, xr, gate, w_out):
    b, r, d = xr.shape
    ts = ROW_TILE
    heads, hv = ot.shape[1], ot.shape[2]
    return pl.pallas_call(
        _outproj_kernel,
        grid=(b, r // ts),
        in_specs=[
            pl.BlockSpec((1, heads, hv, ts), lambda bb, i: (bb, 0, 0, i)),
            pl.BlockSpec((1, ts, heads * hv), lambda bb, i: (bb, i, 0)),
            pl.BlockSpec((1, ts, d), lambda bb, i: (bb, i, 0)),
            _mod_spec(gate),
            pl.BlockSpec(w_out.shape, lambda bb, i: (0, 0)),
        ],
        out_specs=pl.BlockSpec((1, ts, d), lambda bb, i: (bb, i, 0)),
        out_shape=jax.ShapeDtypeStruct((b, r, d), _F32),
        compiler_params=_compiler_params(2),
        name="outproj",
    )(ot, gs, xr, gate, w_out)


def _axial_rope_tables_t(n_tokens, rot_dim):
    rows = n_tokens // GRID_W
    row = jnp.repeat(jnp.arange(rows, dtype=_F32), GRID_W)
    col = jnp.tile(jnp.arange(GRID_W, dtype=_F32), rows)
    axis_dim = rot_dim // 2
    inv_freq = jnp.power(ROPE_THETA, -jnp.arange(0, axis_dim, 2, dtype=_F32) / axis_dim)
    ang_r = row[:, None] * inv_freq[None, :]
    ang_c = col[:, None] * inv_freq[None, :]
    ang = jnp.concatenate([ang_r, ang_r, ang_c, ang_c], axis=-1)
    return jnp.cos(ang).T, jnp.sin(ang).T


def _identity_rope_tables_t(n_tokens, rot_dim):
    return jnp.ones((rot_dim, n_tokens), _F32), jnp.zeros((rot_dim, n_tokens), _F32)


def _split_mod(mod, n_batch):
    d = D_MODEL
    lat = tuple(mod[:n_batch, None, k * d:(k + 1) * d] for k in range(3))
    ctx = tuple(mod[n_batch:n_batch + 1, None, k * d:(k + 1) * d] for k in range(3))
    return lat, ctx


def kernel(x, c, ctx, c_ctx, l0_w_mod, l0_b_mod, l0_norm, l0_w_in, l0_q_norm, l0_k_norm, l0_w_out, l1_w_mod, l1_b_mod, l1_norm, l1_w_in, l1_kv_a_norm, l1_w_kv_b, l1_q_a_norm, l1_w_q_b, l1_q_norm, l1_k_nope_norm, l1_k_rope_norm, l1_w_out):
    n_batch, seq, d = x.shape
    ctx_len = ctx.shape[1]
    cond_rows = 8
    cond = jnp.concatenate([c, c_ctx[None, :], jnp.zeros((cond_rows - n_batch - 1, d), _F32)], axis=0)

    (sh_l, sc_l, gt_l), (sh_c, sc_c, gt_c) = _split_mod(_adaln(cond, l0_w_mod, l0_b_mod), n_batch)
    kw, kv_w = GQA_KV_WIDTH, 2 * GQA_KV_WIDTH
    w_k, w_v = l0_w_in[:, :kw], l0_w_in[:, kw:kv_w]
    w_q, w_g = l0_w_in[:, kv_w:kv_w + GQA_WIDTH], l0_w_in[:, kv_w + GQA_WIDTH:]
    wqkv_t = jnp.concatenate([w_q, w_k, w_v], axis=1).T.astype(_BF16)
    q_scale = GQA_HEAD_DIM ** -0.5 * LOG2E
    qkw = jnp.concatenate([jnp.tile((l0_q_norm * q_scale)[None], (GQA_HEADS, 1)),
                           jnp.tile(l0_k_norm[None], (GQA_KV_HEADS, 1))], axis=0)[:, :, None]
    nw0 = l0_norm[None, :]
    wg0 = w_g.astype(_BF16)
    cos_a, sin_a = _axial_rope_tables_t(seq, GQA_HEAD_DIM)
    cos_i, sin_i = _identity_rope_tables_t(ctx_len, GQA_HEAD_DIM)
    q_l, k_l, v_l, g_l = _gqa_prep(x, sh_l, sc_l, nw0, wqkv_t, wg0, qkw, cos_a, sin_a)
    q_c, k_c, v_c, g_c = _gqa_prep(ctx, sh_c, sc_c, nw0, wqkv_t, wg0, qkw, cos_i, sin_i)
    o_l = _attention(q_l, k_l, v_l, k_c, v_c, heads_per_step=GQA_GROUP, tq=512, name="gqa_attn")
    o_c = _attention(q_c, None, None, k_c, v_c, heads_per_step=GQA_GROUP, tq=ctx_len, name="gqa_attn_ctx")
    w_out0 = l0_w_out.astype(_BF16)
    x = _outproj(o_l, g_l, x, gt_l, w_out0)
    ctx = _outproj(o_c, g_c, ctx, gt_c, w_out0)

    (sh_l, sc_l, gt_l), (sh_c, sc_c, _) = _split_mod(_adaln(cond, l1_w_mod, l1_b_mod), n_batch)
    c_kv, c_kr = MLA_KV_LORA, MLA_KV_LORA + MLA_ROPE
    c_q = c_kr + MLA_Q_LORA
    win = jnp.concatenate([l1_w_in[:, :c_kv], l1_w_in[:, c_kr:c_q], l1_w_in[:, c_q:], l1_w_in[:, c_kv:c_kr],
                           jnp.zeros((d, LANES - MLA_ROPE), _F32)], axis=1).astype(_BF16)
    wkvb = l1_w_kv_b.reshape(MLA_KV_LORA, MLA_HEADS, MLA_NOPE + MLA_V)
    wkvb_t = jnp.concatenate([wkvb[:, :, :MLA_NOPE].reshape(MLA_KV_LORA, -1),
                              wkvb[:, :, MLA_NOPE:].reshape(MLA_KV_LORA, -1)], axis=1).T.astype(_BF16)
    wqb_t = l1_w_q_b.T.astype(_BF16)
    q_scale = MLA_QK ** -0.5 * LOG2E
    nw1 = l1_norm[None, :]
    kvaw, qaw = l1_kv_a_norm[None, :], l1_q_a_norm[None, :]
    knw = l1_k_nope_norm[None, :, None]
    qw = (l1_q_norm * q_scale)[None, :, None]
    krw = l1_k_rope_norm[None, :, None]
    cos_b, sin_b = _axial_rope_tables_t(seq, MLA_ROPE)
    cos_i, sin_i = _identity_rope_tables_t(ctx_len, MLA_ROPE)
    q_l, k_l, v_l, g_l = _mla_prep(x, sh_l, sc_l, nw1, win, kvaw, qaw, wkvb_t, wqb_t, knw, qw, krw, cos_b, sin_b)
    _, k_c, v_c, _ = _mla_prep(ctx, sh_c, sc_c, nw1, win, kvaw, qaw, wkvb_t, wqb_t, knw, qw, krw, cos_i, sin_i)
    o_l = _attention(q_l, k_l, v_l, k_c, v_c, heads_per_step=1, tq=2048, name="mla_attn")
    return _outproj(o_l, g_l, x, gt_l, l1_w_out.astype(_BF16))
```

```python
import functools
import math

import jax
import jax.numpy as jnp
from jax import lax
from jax.experimental import pallas as pl
from jax.experimental.pallas import tpu as pltpu

D_MODEL = 1024
GRID_W = 64
ROPE_THETA = 10000.0
NORM_EPS = 1e-6

GQA_HEADS = 16
GQA_KV_HEADS = 4
GQA_GROUP = GQA_HEADS // GQA_KV_HEADS
GQA_HEAD_DIM = 64
GQA_WIDTH = GQA_HEADS * GQA_HEAD_DIM
GQA_KV_WIDTH = GQA_KV_HEADS * GQA_HEAD_DIM

MLA_HEADS = 16
MLA_NOPE = 64
MLA_ROPE = 32
MLA_V = 64
MLA_Q_LORA = 384
MLA_KV_LORA = 256
MLA_QK = MLA_NOPE + MLA_ROPE
MLA_WIDTH = MLA_HEADS * MLA_V

LANES = 128
MXU_COLS = 256
BF16_SUBLANES = 16
VMEM_LIMIT_BYTES = 48 * 1024 * 1024

ROW_TILE = 256
KV_TILE = ROW_TILE
Q_CHUNK = MXU_COLS
HEAD_V = 64
F32_SUBLANES = 8
K_LANES = LANES
GQA_QT_ROWS = GQA_HEAD_DIM + BF16_SUBLANES
MLA_QT_ROWS = K_LANES
QK_AHEAD = 4
ONLINE_TILES_PER_ITER = 4
LOG2E = math.log2(math.e)
M_INIT = -1e30
SHIFT_MARGIN = 1.0 + 2.0 ** -6
SHIFT_LIMIT = 60.0

_BF16 = jnp.bfloat16
_F32 = jnp.float32


def _silu(v):
    return v * jax.nn.sigmoid(v)


def _compiler_params(n_grid_axes):
    return pltpu.CompilerParams(
        dimension_semantics=("arbitrary",) * n_grid_axes,
        vmem_limit_bytes=VMEM_LIMIT_BYTES,
    )


def _adaln_kernel(cond_ref, w_ref, b_ref, o_ref):
    a = _silu(cond_ref[...])
    o_ref[...] = jnp.dot(a, w_ref[...], preferred_element_type=_F32,
                         precision=lax.Precision.HIGHEST) + b_ref[...]


def _adaln(cond, w_mod, b_mod):
    rows, d = cond.shape
    n = w_mod.shape[1]
    tn = 1024
    return pl.pallas_call(
        _adaln_kernel,
        grid=(n // tn,),
        in_specs=[
            pl.BlockSpec((rows, d), lambda j: (0, 0)),
            pl.BlockSpec((d, tn), lambda j: (0, j)),
            pl.BlockSpec((1, tn), lambda j: (0, j)),
        ],
        out_specs=pl.BlockSpec((rows, tn), lambda j: (0, j)),
        out_shape=jax.ShapeDtypeStruct((rows, n), _F32),
        compiler_params=_compiler_params(1),
        name="adaln",
    )(cond, w_mod, b_mod.reshape(1, n))


def _modulated_norm(x, nw, scale, shift):
    ms = jnp.mean(x * x, axis=-1, keepdims=True)
    return (x * lax.rsqrt(ms + NORM_EPS) * nw) * (1.0 + scale) + shift


def _head_rms_norm_t(t, w):
    ms = jnp.mean(t * t, axis=1, keepdims=True)
    return t * lax.rsqrt(ms + NORM_EPS) * w


def _axial_rope_t(t, cos, sin):
    q = t.shape[1] // 4
    x1, x2, x3, x4 = t[:, 0:q], t[:, q:2 * q], t[:, 2 * q:3 * q], t[:, 3 * q:4 * q]
    rot = jnp.concatenate([-x2, x1, -x4, x3], axis=1)
    return t * cos[None] + rot * sin[None]


def _tail_rows(first_row, n_rows):
    heads, _, cols = first_row.shape
    row = lax.broadcasted_iota(jnp.int32, (heads, n_rows, cols), 1)
    return jnp.where(row == 0, jnp.broadcast_to(first_row, (heads, n_rows, cols)), 0.0)


def _neg_shift_row(q, kb):
    return -(jnp.sqrt(jnp.sum(q * q, axis=1, keepdims=True)) * kb)


def _gqa_prep_kernel(x_ref, shift_ref, scale_ref, nw_ref, wqkv_ref, wg_ref, qkw_ref, kb_ref, cos_ref, sin_ref,
                     q_ref, k_ref, v_ref, g_ref):
    ts = x_ref.shape[1]
    h = _modulated_norm(x_ref[0], nw_ref[...], scale_ref[0], shift_ref[0])
    g = jnp.dot(h.astype(_BF16), wg_ref[...], preferred_element_type=_F32)
    g_ref[0] = _silu(g).astype(_BF16)
    ht = h.T.astype(_BF16)
    a = jnp.dot(wqkv_ref[...], ht, preferred_element_type=_F32)
    n_qk = GQA_HEADS + GQA_KV_HEADS
    qk = a[:n_qk * GQA_HEAD_DIM].reshape(n_qk, GQA_HEAD_DIM, ts)
    qk = _head_rms_norm_t(qk, qkw_ref[...])
    qk = _axial_rope_t(qk, cos_ref[...], sin_ref[...])
    q = qk[:GQA_HEADS]
    q_tail = _tail_rows(_neg_shift_row(q, kb_ref[...]), GQA_QT_ROWS - GQA_HEAD_DIM)
    q_ref[0] = jnp.concatenate([q, q_tail], axis=1).astype(_BF16)
    k_tail = _tail_rows(jnp.ones((GQA_KV_HEADS, 1, ts), _F32), K_LANES - GQA_HEAD_DIM)
    kn = jnp.concatenate([qk[GQA_HEADS:], k_tail], axis=1).reshape(GQA_KV_HEADS * K_LANES, ts).T
    for hh in range(GQA_KV_HEADS):
        k_ref[0, hh] = kn[:, hh * K_LANES:(hh + 1) * K_LANES].astype(_BF16)
    vt = a[n_qk * GQA_HEAD_DIM:].reshape(GQA_KV_HEADS, GQA_HEAD_DIM, ts)
    v_ref[0, :, 0] = vt.astype(_BF16)


def _mod_spec(arr):
    per_batch = arr.shape[0] > 1
    return pl.BlockSpec((1, 1, D_MODEL), lambda b, i: (b if per_batch else 0, 0, 0))


def _gqa_prep(xr, shift, scale, nw, wqkv_t, wg, qkw, kb, cos_t, sin_t):
    b, r, d = xr.shape
    ts = ROW_TILE
    nt = r // ts
    return pl.pallas_call(
        _gqa_prep_kernel,
        grid=(b, nt),
        in_specs=[
            pl.BlockSpec((1, ts, d), lambda bb, i: (bb, i, 0)),
            _mod_spec(shift), _mod_spec(scale),
            pl.BlockSpec((1, d), lambda bb, i: (0, 0)),
            pl.BlockSpec(wqkv_t.shape, lambda bb, i: (0, 0)),
            pl.BlockSpec(wg.shape, lambda bb, i: (0, 0)),
            pl.BlockSpec(qkw.shape, lambda bb, i: (0, 0, 0)),
            pl.BlockSpec(kb.shape, lambda bb, i: (0, 0, 0)),
            pl.BlockSpec((GQA_HEAD_DIM, ts), lambda bb, i: (0, i)),
            pl.BlockSpec((GQA_HEAD_DIM, ts), lambda bb, i: (0, i)),
        ],
        out_specs=[
            pl.BlockSpec((1, GQA_HEADS, GQA_QT_ROWS, ts), lambda bb, i: (bb, 0, 0, i)),
            pl.BlockSpec((1, GQA_KV_HEADS, ts, K_LANES), lambda bb, i: (bb, 0, i, 0)),
            pl.BlockSpec((1, GQA_KV_HEADS, 1, HEAD_V, ts), lambda bb, i: (bb, 0, i, 0, 0)),
            pl.BlockSpec((1, ts, GQA_WIDTH), lambda bb, i: (bb, i, 0)),
        ],
        out_shape=[
            jax.ShapeDtypeStruct((b, GQA_HEADS, GQA_QT_ROWS, r), _BF16),
            jax.ShapeDtypeStruct((b, GQA_KV_HEADS, r, K_LANES), _BF16),
            jax.ShapeDtypeStruct((b, GQA_KV_HEADS, nt, HEAD_V, ts), _BF16),
            jax.ShapeDtypeStruct((b, r, GQA_WIDTH), _BF16),
        ],
        compiler_params=_compiler_params(2),
        name="gqa_prep",
    )(xr, shift, scale, nw, wqkv_t, wg, qkw, kb, cos_t, sin_t)


def _mla_prep_kernel(x_ref, shift_ref, scale_ref, nw_ref, win_ref, kvaw_ref, qaw_ref, wkvb_ref, wqb_ref,
                     knw_ref, qw_ref, krw_ref, kb_ref, cos_ref, sin_ref, q_ref, k_ref, v_ref, g_ref):
    ts = x_ref.shape[1]
    h = _modulated_norm(x_ref[0], nw_ref[...], scale_ref[0], shift_ref[0])
    p = jnp.dot(h.astype(_BF16), win_ref[...], preferred_element_type=_F32)
    c0, c1, c2 = MLA_KV_LORA, MLA_KV_LORA + MLA_Q_LORA, MLA_KV_LORA + MLA_Q_LORA + MLA_WIDTH
    kv_a, q_a, g, kr = p[:, :c0], p[:, c0:c1], p[:, c1:c2], p[:, c2:]
    g_ref[0] = _silu(g).astype(_BF16)

    def row_norm(t, w):
        return t * lax.rsqrt(jnp.mean(t * t, axis=-1, keepdims=True) + NORM_EPS) * w

    kv_an = row_norm(kv_a, kvaw_ref[...])
    q_an = row_norm(q_a, qaw_ref[...])
    kvt = jnp.dot(wkvb_ref[...], kv_an.T.astype(_BF16), preferred_element_type=_F32)
    kn = _head_rms_norm_t(kvt[:MLA_HEADS * MLA_NOPE].reshape(MLA_HEADS, MLA_NOPE, ts), knw_ref[...])
    vt = kvt[MLA_HEADS * MLA_NOPE:].reshape(MLA_HEADS, MLA_V, ts)
    ones_row = jnp.ones((MLA_HEADS, 1, ts), _F32)
    v_ref[0, :, 0] = vt.astype(_BF16)

    qt = jnp.dot(wqb_ref[...], q_an.T.astype(_BF16), preferred_element_type=_F32)
    qt = _head_rms_norm_t(qt.reshape(MLA_HEADS, MLA_QK, ts), qw_ref[...])
    q_nope = qt[:, :MLA_NOPE]
    q_rope = _axial_rope_t(qt[:, MLA_NOPE:], cos_ref[...], sin_ref[...])
    neg_b = -(jnp.sqrt(jnp.sum(q_nope * q_nope, axis=1, keepdims=True)
                       + jnp.sum(q_rope * q_rope, axis=1, keepdims=True)) * kb_ref[...])
    q_ref[0] = jnp.concatenate([q_nope, q_rope, _tail_rows(neg_b, MLA_QT_ROWS - MLA_QK)], axis=1).astype(_BF16)

    krt = kr.T[:MLA_ROPE].reshape(1, MLA_ROPE, ts)
    krt = _axial_rope_t(_head_rms_norm_t(krt, krw_ref[...]), cos_ref[...], sin_ref[...])
    kft = jnp.concatenate([kn, jnp.broadcast_to(krt, (MLA_HEADS, MLA_ROPE, ts)),
                           _tail_rows(ones_row, K_LANES - MLA_QK)], axis=1)
    kf = kft.reshape(MLA_HEADS * K_LANES, ts).T
    for hh in range(MLA_HEADS):
        k_ref[0, hh] = kf[:, hh * K_LANES:(hh + 1) * K_LANES].astype(_BF16)


def _mla_prep(xr, shift, scale, nw, win, kvaw, qaw, wkvb_t, wqb_t, knw, qw, krw, kb, cos_t, sin_t):
    b, r, d = xr.shape
    ts = ROW_TILE
    nt = r // ts
    const2 = lambda bb, i: (0, 0)
    const3 = lambda bb, i: (0, 0, 0)
    return pl.pallas_call(
        _mla_prep_kernel,
        grid=(b, nt),
        in_specs=[
            pl.BlockSpec((1, ts, d), lambda bb, i: (bb, i, 0)),
            _mod_spec(shift), _mod_spec(scale),
            pl.BlockSpec((1, d), const2),
            pl.BlockSpec(win.shape, const2),
            pl.BlockSpec(kvaw.shape, const2),
            pl.BlockSpec(qaw.shape, const2),
            pl.BlockSpec(wkvb_t.shape, const2),
            pl.BlockSpec(wqb_t.shape, const2),
            pl.BlockSpec(knw.shape, const3),
            pl.BlockSpec(qw.shape, const3),
            pl.BlockSpec(krw.shape, const3),
            pl.BlockSpec(kb.shape, const3),
            pl.BlockSpec((MLA_ROPE, ts), lambda bb, i: (0, i)),
            pl.BlockSpec((MLA_ROPE, ts), lambda bb, i: (0, i)),
        ],
        out_specs=[
            pl.BlockSpec((1, MLA_HEADS, MLA_QT_ROWS, ts), lambda bb, i: (bb, 0, 0, i)),
            pl.BlockSpec((1, MLA_HEADS, ts, K_LANES), lambda bb, i: (bb, 0, i, 0)),
            pl.BlockSpec((1, MLA_HEADS, 1, HEAD_V, ts), lambda bb, i: (bb, 0, i, 0, 0)),
            pl.BlockSpec((1, ts, MLA_WIDTH), lambda bb, i: (bb, i, 0)),
        ],
        out_shape=[
            jax.ShapeDtypeStruct((b, MLA_HEADS, MLA_QT_ROWS, r), _BF16),
            jax.ShapeDtypeStruct((b, MLA_HEADS, r, K_LANES), _BF16),
            jax.ShapeDtypeStruct((b, MLA_HEADS, nt, HEAD_V, ts), _BF16),
            jax.ShapeDtypeStruct((b, r, MLA_WIDTH), _BF16),
        ],
        compiler_params=_compiler_params(2),
        name="mla_prep",
    )(xr, shift, scale, nw, win, kvaw, qaw, wkvb_t, wqb_t, knw, qw, krw, kb, cos_t, sin_t)


def _attn_kernel(*refs, heads, chunks_per_head, n_lat_tiles, n_ctx_tiles):
    if n_lat_tiles:
        flag_ref, q_ref, kl_ref, vl_ref, kc_ref, vc_ref, o_ref, m_ref, l_ref, acc_ref = refs
    else:
        flag_ref, q_ref, kc_ref, vc_ref, o_ref, m_ref, l_ref, acc_ref = refs
    n_chunks = heads * chunks_per_head
    qk_rows = q_ref.shape[2]

    def group_sums(p):
        return jnp.sum(p.reshape(KV_TILE // F32_SUBLANES, F32_SUBLANES, Q_CHUNK), axis=0)

    def q_chunk(c):
        hh, cc = divmod(c, chunks_per_head)
        return hh, slice(cc * Q_CHUNK, (cc + 1) * Q_CHUNK)

    ctx_tiles = [(kc_ref, vc_ref, j) for j in range(n_ctx_tiles)]

    def scores(tile, c):
        k_ref, _, j = tile
        hh, cols = q_chunk(c)
        start = j * KV_TILE
        if not isinstance(j, int):
            start = pl.multiple_of(start, KV_TILE)
        kt = k_ref[0, 0, pl.ds(start, KV_TILE), :qk_rows]
        return jnp.dot(kt, q_ref[0, hh, :, cols], preferred_element_type=_F32)

    def pipelined(steps, consume):
        pending = [scores(tile, c) for tile, c in steps[:QK_AHEAD]]
        for t, (tile, c) in enumerate(steps):
            s = pending.pop(0)
            if t + QK_AHEAD < len(steps):
                pending.append(scores(*steps[t + QK_AHEAD]))
            consume(t, tile, c, s)

    def shift_softmax(tiles):
        part = [None, None]

        def consume(t, tile, c, s):
            p = jnp.exp2(s)
            pv = jnp.dot(tile[1][0, 0, tile[2]], p.astype(_BF16), preferred_element_type=_F32)
            first, last = t % len(tiles) == 0, t % len(tiles) == len(tiles) - 1
            part[0] = pv if first else part[0] + pv
            part[1] = group_sums(p) if first else part[1] + group_sums(p)
            if last:
                acc_ref[c] = part[0]
                l_ref[c] = part[1]

        pipelined([(tile, c) for c in range(n_chunks) for tile in tiles], consume)

    def online_softmax_update(tiles):
        def consume(t, tile, c, s):
            m_prev = m_ref[c]
            m_new = jnp.maximum(m_prev, jnp.max(s, axis=0, keepdims=True))
            alpha = jnp.exp2(m_prev - m_new)
            p = jnp.exp2(s - m_new)
            pv = jnp.dot(tile[1][0, 0, tile[2]], p.astype(_BF16), preferred_element_type=_F32)
            acc_ref[c] = alpha * acc_ref[c] + pv
            l_ref[c] = alpha * l_ref[c] + group_sums(p)
            m_ref[c] = m_new

        pipelined([(tile, c) for tile in tiles for c in range(n_chunks)], consume)

    shift_is_safe = flag_ref[0] != 0

    @pl.when(shift_is_safe)
    def _():
        lat_tiles = [(kl_ref, vl_ref, j) for j in range(n_lat_tiles)] if n_lat_tiles else []
        shift_softmax(lat_tiles + ctx_tiles)

    @pl.when(jnp.logical_not(shift_is_safe))
    def _():
        m_ref[...] = jnp.full(m_ref.shape, M_INIT, _F32)
        acc_ref[...] = jnp.zeros(acc_ref.shape, _F32)
        l_ref[...] = jnp.zeros(l_ref.shape, _F32)
        if n_lat_tiles:
            def body(jj, carry):
                online_softmax_update([(kl_ref, vl_ref, jj * ONLINE_TILES_PER_ITER + u)
                                       for u in range(ONLINE_TILES_PER_ITER)])
                return carry
            lax.fori_loop(0, n_lat_tiles // ONLINE_TILES_PER_ITER, body, 0)
        online_softmax_update(ctx_tiles)

    for c in range(n_chunks):
        hh, cols = q_chunk(c)
        denom = jnp.sum(l_ref[c], axis=0, keepdims=True)
        o_ref[0, hh, :, cols] = (acc_ref[c] / denom).astype(_BF16)


def _attention(shift_ok, qt, k_lat, v_lat, k_ctx, v_ctx, *, heads_per_step, tq, name):
    b, h, qk_rows, r = qt.shape
    kvh = k_ctx.shape[1]
    n_ctx_tiles = v_ctx.shape[2]
    n_lat_tiles = v_lat.shape[2] if v_lat is not None else 0
    chunks_per_head = tq // Q_CHUNK
    n_chunks = heads_per_step * chunks_per_head

    def kv_specs(k, v):
        return [pl.BlockSpec((1, 1) + k.shape[2:], lambda bb, g, i: (bb, g, 0, 0)),
                pl.BlockSpec((1, 1) + v.shape[2:], lambda bb, g, i: (bb, g, 0, 0, 0))]

    in_specs = [pl.BlockSpec(memory_space=pltpu.SMEM),
                pl.BlockSpec((1, heads_per_step, qk_rows, tq), lambda bb, g, i: (bb, g, 0, i))]
    args = [shift_ok, qt]
    if n_lat_tiles:
        in_specs += kv_specs(k_lat, v_lat)
        args += [k_lat, v_lat]
    in_specs += kv_specs(k_ctx, v_ctx)
    args += [k_ctx, v_ctx]
    return pl.pallas_call(
        functools.partial(_attn_kernel, heads=heads_per_step, chunks_per_head=chunks_per_head,
                          n_lat_tiles=n_lat_tiles, n_ctx_tiles=n_ctx_tiles),
        grid=(b, kvh, r // tq),
        in_specs=in_specs,
        out_specs=pl.BlockSpec((1, heads_per_step, HEAD_V, tq), lambda bb, g, i: (bb, g, 0, i)),
        out_shape=jax.ShapeDtypeStruct((b, h, HEAD_V, r), _BF16),
        scratch_shapes=[pltpu.VMEM((n_chunks, 1, Q_CHUNK), _F32),
                        pltpu.VMEM((n_chunks, F32_SUBLANES, Q_CHUNK), _F32),
                        pltpu.VMEM((n_chunks, HEAD_V, Q_CHUNK), _F32)],
        compiler_params=_compiler_params(3),
        name=name,
    )(*args)


def _outproj_kernel(o_ref, g_ref, x_ref, gate_ref, w_ref, y_ref):
    ts = x_ref.shape[1]
    o = o_ref[0].astype(_F32).reshape(o_ref.shape[1] * o_ref.shape[2], ts).T
    z = (o * g_ref[0].astype(_F32)).astype(_BF16)
    y = jnp.dot(z, w_ref[...], preferred_element_type=_F32)
    y_ref[0] = x_ref[0] + gate_ref[0] * y


def _outproj(ot, gs, xr, gate, w_out):
    b, r, d = xr.shape
    ts = ROW_TILE
    heads, hv = ot.shape[1], ot.shape[2]
    return pl.pallas_call(
        _outproj_kernel,
        grid=(b, r // ts),
        in_specs=[
            pl.BlockSpec((1, heads, hv, ts), lambda bb, i: (bb, 0, 0, i)),
            pl.BlockSpec((1, ts, heads * hv), lambda bb, i: (bb, i, 0)),
            pl.BlockSpec((1, ts, d), lambda bb, i: (bb, i, 0)),
            _mod_spec(gate),
            pl.BlockSpec(w_out.shape, lambda bb, i: (0, 0)),
        ],
        out_specs=pl.BlockSpec((1, ts, d), lambda bb, i: (bb, i, 0)),
        out_shape=jax.ShapeDtypeStruct((b, r, d), _F32),
        compiler_params=_compiler_params(2),
        name="outproj",
    )(ot, gs, xr, gate, w_out)


def _axial_rope_tables_t(n_tokens, rot_dim):
    rows = n_tokens // GRID_W
    row = jnp.repeat(jnp.arange(rows, dtype=_F32), GRID_W)
    col = jnp.tile(jnp.arange(GRID_W, dtype=_F32), rows)
    axis_dim = rot_dim // 2
    inv_freq = jnp.power(ROPE_THETA, -jnp.arange(0, axis_dim, 2, dtype=_F32) / axis_dim)
    ang_r = row[:, None] * inv_freq[None, :]
    ang_c = col[:, None] * inv_freq[None, :]
    ang = jnp.concatenate([ang_r, ang_r, ang_c, ang_c], axis=-1)
    return jnp.cos(ang).T, jnp.sin(ang).T


def _identity_rope_tables_t(n_tokens, rot_dim):
    return jnp.ones((rot_dim, n_tokens), _F32), jnp.zeros((rot_dim, n_tokens), _F32)


def _split_mod(mod, n_batch):
    d = D_MODEL
    lat = tuple(mod[:n_batch, None, k * d:(k + 1) * d] for k in range(3))
    ctx = tuple(mod[n_batch:n_batch + 1, None, k * d:(k + 1) * d] for k in range(3))
    return lat, ctx


def _shift_params(q_norm_bound, k_norm_bound):
    kb = k_norm_bound * SHIFT_MARGIN
    ok = (q_norm_bound * kb <= SHIFT_LIMIT).astype(jnp.int32)
    return kb.reshape(1, 1, 1), ok.reshape(1)


def kernel(x, c, ctx, c_ctx, l0_w_mod, l0_b_mod, l0_norm, l0_w_in, l0_q_norm, l0_k_norm, l0_w_out, l1_w_mod, l1_b_mod, l1_norm, l1_w_in, l1_kv_a_norm, l1_w_kv_b, l1_q_a_norm, l1_w_q_b, l1_q_norm, l1_k_nope_norm, l1_k_rope_norm, l1_w_out):
    n_batch, seq, d = x.shape
    ctx_len = ctx.shape[1]
    cond_rows = 8
    cond = jnp.concatenate([c, c_ctx[None, :], jnp.zeros((cond_rows - n_batch - 1, d), _F32)], axis=0)

    (sh_l, sc_l, gt_l), (sh_c, sc_c, gt_c) = _split_mod(_adaln(cond, l0_w_mod, l0_b_mod), n_batch)
    kw, kv_w = GQA_KV_WIDTH, 2 * GQA_KV_WIDTH
    w_k, w_v = l0_w_in[:, :kw], l0_w_in[:, kw:kv_w]
    w_q, w_g = l0_w_in[:, kv_w:kv_w + GQA_WIDTH], l0_w_in[:, kv_w + GQA_WIDTH:]
    wqkv_t = jnp.concatenate([w_q, w_k, w_v], axis=1).T.astype(_BF16)
    q_scale = GQA_HEAD_DIM ** -0.5 * LOG2E
    qkw = jnp.concatenate([jnp.tile((l0_q_norm * q_scale)[None], (GQA_HEADS, 1)),
                           jnp.tile(l0_k_norm[None], (GQA_KV_HEADS, 1))], axis=0)[:, :, None]
    kb0, ok0 = _shift_params(GQA_HEAD_DIM ** 0.5 * q_scale * jnp.max(jnp.abs(l0_q_norm)),
                             GQA_HEAD_DIM ** 0.5 * jnp.max(jnp.abs(l0_k_norm)))
    nw0 = l0_norm[None, :]
    wg0 = w_g.astype(_BF16)
    cos_a, sin_a = _axial_rope_tables_t(seq, GQA_HEAD_DIM)
    cos_i, sin_i = _identity_rope_tables_t(ctx_len, GQA_HEAD_DIM)
    q_l, k_l, v_l, g_l = _gqa_prep(x, sh_l, sc_l, nw0, wqkv_t, wg0, qkw, kb0, cos_a, sin_a)
    q_c, k_c, v_c, g_c = _gqa_prep(ctx, sh_c, sc_c, nw0, wqkv_t, wg0, qkw, kb0, cos_i, sin_i)
    o_l = _attention(ok0, q_l, k_l, v_l, k_c, v_c, heads_per_step=GQA_GROUP, tq=512, name="gqa_attn")
    o_c = _attention(ok0, q_c, None, None, k_c, v_c, heads_per_step=GQA_GROUP, tq=ctx_len, name="gqa_attn_ctx")
    w_out0 = l0_w_out.astype(_BF16)
    x = _outproj(o_l, g_l, x, gt_l, w_out0)
    ctx = _outproj(o_c, g_c, ctx, gt_c, w_out0)

    (sh_l, sc_l, gt_l), (sh_c, sc_c, _) = _split_mod(_adaln(cond, l1_w_mod, l1_b_mod), n_batch)
    c_kv, c_kr = MLA_KV_LORA, MLA_KV_LORA + MLA_ROPE
    c_q = c_kr + MLA_Q_LORA
    win = jnp.concatenate([l1_w_in[:, :c_kv], l1_w_in[:, c_kr:c_q], l1_w_in[:, c_q:], l1_w_in[:, c_kv:c_kr],
                           jnp.zeros((d, LANES - MLA_ROPE), _F32)], axis=1).astype(_BF16)
    wkvb = l1_w_kv_b.reshape(MLA_KV_LORA, MLA_HEADS, MLA_NOPE + MLA_V)
    wkvb_t = jnp.concatenate([wkvb[:, :, :MLA_NOPE].reshape(MLA_KV_LORA, -1),
                              wkvb[:, :, MLA_NOPE:].reshape(MLA_KV_LORA, -1)], axis=1).T.astype(_BF16)
    wqb_t = l1_w_q_b.T.astype(_BF16)
    q_scale = MLA_QK ** -0.5 * LOG2E
    nw1 = l1_norm[None, :]
    kvaw, qaw = l1_kv_a_norm[None, :], l1_q_a_norm[None, :]
    knw = l1_k_nope_norm[None, :, None]
    qw = (l1_q_norm * q_scale)[None, :, None]
    krw = l1_k_rope_norm[None, :, None]
    kb1, ok1 = _shift_params(MLA_QK ** 0.5 * q_scale * jnp.max(jnp.abs(l1_q_norm)),
                             jnp.sqrt(MLA_NOPE * jnp.max(jnp.square(l1_k_nope_norm))
                                      + MLA_ROPE * jnp.max(jnp.square(l1_k_rope_norm))))
    cos_b, sin_b = _axial_rope_tables_t(seq, MLA_ROPE)
    cos_i, sin_i = _identity_rope_tables_t(ctx_len, MLA_ROPE)
    q_l, k_l, v_l, g_l = _mla_prep(x, sh_l, sc_l, nw1, win, kvaw, qaw, wkvb_t, wqb_t, knw, qw, krw, kb1,
                                   cos_b, sin_b)
    _, k_c, v_c, _ = _mla_prep(ctx, sh_c, sc_c, nw1, win, kvaw, qaw, wkvb_t, wqb_t, knw, qw, krw, kb1,
                               cos_i, sin_i)
    o_l = _attention(ok1, q_l, k_l, v_l, k_c, v_c, heads_per_step=1, tq=2048, name="mla_attn")
    return _outproj(o_l, g_l, x, gt_l, l1_w_out.astype(_BF16))
```

```python
import functools
import math

import jax
import jax.numpy as jnp
from jax import lax
from jax.experimental import pallas as pl
from jax.experimental.pallas import tpu as pltpu

D_MODEL = 1024
GRID_W = 64
ROPE_THETA = 10000.0
NORM_EPS = 1e-6

GQA_HEADS = 16
GQA_KV_HEADS = 4
GQA_GROUP = GQA_HEADS // GQA_KV_HEADS
GQA_HEAD_DIM = 64
GQA_WIDTH = GQA_HEADS * GQA_HEAD_DIM
GQA_KV_WIDTH = GQA_KV_HEADS * GQA_HEAD_DIM

MLA_HEADS = 16
MLA_NOPE = 64
MLA_ROPE = 32
MLA_V = 64
MLA_Q_LORA = 384
MLA_KV_LORA = 256
MLA_QK = MLA_NOPE + MLA_ROPE
MLA_WIDTH = MLA_HEADS * MLA_V

LANES = 128
MXU_COLS = 256
BF16_SUBLANES = 16
VMEM_LIMIT_BYTES = 48 * 1024 * 1024

ROW_TILE = 256
KV_TILE = ROW_TILE
Q_CHUNK = MXU_COLS
HEAD_V = 64
F32_SUBLANES = 8
K_LANES = LANES
GQA_QT_ROWS = GQA_HEAD_DIM + BF16_SUBLANES
MLA_QT_ROWS = K_LANES
QK_AHEAD = 4
CHUNKS_IN_FLIGHT = 2
ONLINE_TILES_PER_ITER = 4
LOG2E = math.log2(math.e)
M_INIT = -1e30
SHIFT_MARGIN = 1.0 + 2.0 ** -6
SHIFT_LIMIT = 60.0

_BF16 = jnp.bfloat16
_F32 = jnp.float32


def _silu(v):
    return v * jax.nn.sigmoid(v)


def _compiler_params(n_grid_axes, flags=None):
    return pltpu.CompilerParams(
        dimension_semantics=("arbitrary",) * n_grid_axes,
        vmem_limit_bytes=VMEM_LIMIT_BYTES,
        flags=flags,
    )


def _adaln_kernel(cond_ref, w_ref, b_ref, o_ref):
    a = _silu(cond_ref[...])
    o_ref[...] = jnp.dot(a, w_ref[...], preferred_element_type=_F32,
                         precision=lax.Precision.HIGHEST) + b_ref[...]


def _adaln(cond, w_mod, b_mod):
    rows, d = cond.shape
    n = w_mod.shape[1]
    tn = 1024
    return pl.pallas_call(
        _adaln_kernel,
        grid=(n // tn,),
        in_specs=[
            pl.BlockSpec((rows, d), lambda j: (0, 0)),
            pl.BlockSpec((d, tn), lambda j: (0, j)),
            pl.BlockSpec((1, tn), lambda j: (0, j)),
        ],
        out_specs=pl.BlockSpec((rows, tn), lambda j: (0, j)),
        out_shape=jax.ShapeDtypeStruct((rows, n), _F32),
        compiler_params=_compiler_params(1),
        name="adaln",
    )(cond, w_mod, b_mod.reshape(1, n))


def _modulated_norm(x, nw, scale, shift):
    ms = jnp.mean(x * x, axis=-1, keepdims=True)
    return (x * lax.rsqrt(ms + NORM_EPS) * nw) * (1.0 + scale) + shift


def _head_rms_norm_t(t, w):
    ms = jnp.mean(t * t, axis=1, keepdims=True)
    return t * lax.rsqrt(ms + NORM_EPS) * w


def _axial_rope_t(t, cos, sin):
    q = t.shape[1] // 4
    x1, x2, x3, x4 = t[:, 0:q], t[:, q:2 * q], t[:, 2 * q:3 * q], t[:, 3 * q:4 * q]
    rot = jnp.concatenate([-x2, x1, -x4, x3], axis=1)
    return t * cos[None] + rot * sin[None]


def _tail_rows(first_row, n_rows):
    heads, _, cols = first_row.shape
    row = lax.broadcasted_iota(jnp.int32, (heads, n_rows, cols), 1)
    return jnp.where(row == 0, jnp.broadcast_to(first_row, (heads, n_rows, cols)), 0.0)


def _neg_shift_row(q, kb):
    return -(jnp.sqrt(jnp.sum(q * q, axis=1, keepdims=True)) * kb)


def _gqa_prep_kernel(x_ref, shift_ref, scale_ref, nw_ref, wqkv_ref, wg_ref, qkw_ref, kb_ref, cos_ref, sin_ref,
                     q_ref, k_ref, v_ref, g_ref):
    ts = x_ref.shape[1]
    h = _modulated_norm(x_ref[0], nw_ref[...], scale_ref[0], shift_ref[0])
    g = jnp.dot(h.astype(_BF16), wg_ref[...], preferred_element_type=_F32)
    g_ref[0] = _silu(g).astype(_BF16)
    ht = h.T.astype(_BF16)
    a = jnp.dot(wqkv_ref[...], ht, preferred_element_type=_F32)
    n_qk = GQA_HEADS + GQA_KV_HEADS
    qk = a[:n_qk * GQA_HEAD_DIM].reshape(n_qk, GQA_HEAD_DIM, ts)
    qk = _head_rms_norm_t(qk, qkw_ref[...])
    qk = _axial_rope_t(qk, cos_ref[...], sin_ref[...])
    q = qk[:GQA_HEADS]
    q_tail = _tail_rows(_neg_shift_row(q, kb_ref[...]), GQA_QT_ROWS - GQA_HEAD_DIM)
    q_ref[0] = jnp.concatenate([q, q_tail], axis=1).astype(_BF16)
    k_tail = _tail_rows(jnp.ones((GQA_KV_HEADS, 1, ts), _F32), K_LANES - GQA_HEAD_DIM)
    kn = jnp.concatenate([qk[GQA_HEADS:], k_tail], axis=1).reshape(GQA_KV_HEADS * K_LANES, ts).T
    for hh in range(GQA_KV_HEADS):
        k_ref[0, hh] = kn[:, hh * K_LANES:(hh + 1) * K_LANES].astype(_BF16)
    vt = a[n_qk * GQA_HEAD_DIM:].reshape(GQA_KV_HEADS, GQA_HEAD_DIM, ts)
    v_ref[0, :, 0] = vt.astype(_BF16)


def _mod_spec(arr):
    per_batch = arr.shape[0] > 1
    return pl.BlockSpec((1, 1, D_MODEL), lambda b, i: (b if per_batch else 0, 0, 0))


def _gqa_prep(xr, shift, scale, nw, wqkv_t, wg, qkw, kb, cos_t, sin_t):
    b, r, d = xr.shape
    ts = ROW_TILE
    nt = r // ts
    return pl.pallas_call(
        _gqa_prep_kernel,
        grid=(b, nt),
        in_specs=[
            pl.BlockSpec((1, ts, d), lambda bb, i: (bb, i, 0)),
            _mod_spec(shift), _mod_spec(scale),
            pl.BlockSpec((1, d), lambda bb, i: (0, 0)),
            pl.BlockSpec(wqkv_t.shape, lambda bb, i: (0, 0)),
            pl.BlockSpec(wg.shape, lambda bb, i: (0, 0)),
            pl.BlockSpec(qkw.shape, lambda bb, i: (0, 0, 0)),
            pl.BlockSpec(kb.shape, lambda bb, i: (0, 0, 0)),
            pl.BlockSpec((GQA_HEAD_DIM, ts), lambda bb, i: (0, i)),
            pl.BlockSpec((GQA_HEAD_DIM, ts), lambda bb, i: (0, i)),
        ],
        out_specs=[
            pl.BlockSpec((1, GQA_HEADS, GQA_QT_ROWS, ts), lambda bb, i: (bb, 0, 0, i)),
            pl.BlockSpec((1, GQA_KV_HEADS, ts, K_LANES), lambda bb, i: (bb, 0, i, 0)),
            pl.BlockSpec((1, GQA_KV_HEADS, 1, HEAD_V, ts), lambda bb, i: (bb, 0, i, 0, 0)),
            pl.BlockSpec((1, ts, GQA_WIDTH), lambda bb, i: (bb, i, 0)),
        ],
        out_shape=[
            jax.ShapeDtypeStruct((b, GQA_HEADS, GQA_QT_ROWS, r), _BF16),
            jax.ShapeDtypeStruct((b, GQA_KV_HEADS, r, K_LANES), _BF16),
            jax.ShapeDtypeStruct((b, GQA_KV_HEADS, nt, HEAD_V, ts), _BF16),
            jax.ShapeDtypeStruct((b, r, GQA_WIDTH), _BF16),
        ],
        compiler_params=_compiler_params(2),
        name="gqa_prep",
    )(xr, shift, scale, nw, wqkv_t, wg, qkw, kb, cos_t, sin_t)


def _mla_prep_kernel(x_ref, shift_ref, scale_ref, nw_ref, win_ref, kvaw_ref, qaw_ref, wkvb_ref, wqb_ref,
                     knw_ref, qw_ref, krw_ref, kb_ref, cos_ref, sin_ref, q_ref, k_ref, v_ref, g_ref):
    ts = x_ref.shape[1]
    h = _modulated_norm(x_ref[0], nw_ref[...], scale_ref[0], shift_ref[0])
    p = jnp.dot(h.astype(_BF16), win_ref[...], preferred_element_type=_F32)
    c0, c1, c2 = MLA_KV_LORA, MLA_KV_LORA + MLA_Q_LORA, MLA_KV_LORA + MLA_Q_LORA + MLA_WIDTH
    kv_a, q_a, g, kr = p[:, :c0], p[:, c0:c1], p[:, c1:c2], p[:, c2:]
    g_ref[0] = _silu(g).astype(_BF16)

    def row_norm(t, w):
        return t * lax.rsqrt(jnp.mean(t * t, axis=-1, keepdims=True) + NORM_EPS) * w

    kv_an = row_norm(kv_a, kvaw_ref[...])
    q_an = row_norm(q_a, qaw_ref[...])
    kvt = jnp.dot(wkvb_ref[...], kv_an.T.astype(_BF16), preferred_element_type=_F32)
    kn = _head_rms_norm_t(kvt[:MLA_HEADS * MLA_NOPE].reshape(MLA_HEADS, MLA_NOPE, ts), knw_ref[...])
    vt = kvt[MLA_HEADS * MLA_NOPE:].reshape(MLA_HEADS, MLA_V, ts)
    ones_row = jnp.ones((MLA_HEADS, 1, ts), _F32)
    v_ref[0, :, 0] = vt.astype(_BF16)

    qt = jnp.dot(wqb_ref[...], q_an.T.astype(_BF16), preferred_element_type=_F32)
    qt = _head_rms_norm_t(qt.reshape(MLA_HEADS, MLA_QK, ts), qw_ref[...])
    q_nope = qt[:, :MLA_NOPE]
    q_rope = _axial_rope_t(qt[:, MLA_NOPE:], cos_ref[...], sin_ref[...])
    neg_b = -(jnp.sqrt(jnp.sum(q_nope * q_nope, axis=1, keepdims=True)
                       + jnp.sum(q_rope * q_rope, axis=1, keepdims=True)) * kb_ref[...])
    q_ref[0] = jnp.concatenate([q_nope, q_rope, _tail_rows(neg_b, MLA_QT_ROWS - MLA_QK)], axis=1).astype(_BF16)

    krt = kr.T[:MLA_ROPE].reshape(1, MLA_ROPE, ts)
    krt = _axial_rope_t(_head_rms_norm_t(krt, krw_ref[...]), cos_ref[...], sin_ref[...])
    kft = jnp.concatenate([kn, jnp.broadcast_to(krt, (MLA_HEADS, MLA_ROPE, ts)),
                           _tail_rows(ones_row, K_LANES - MLA_QK)], axis=1)
    kf = kft.reshape(MLA_HEADS * K_LANES, ts).T
    for hh in range(MLA_HEADS):
        k_ref[0, hh] = kf[:, hh * K_LANES:(hh + 1) * K_LANES].astype(_BF16)


def _mla_prep(xr, shift, scale, nw, win, kvaw, qaw, wkvb_t, wqb_t, knw, qw, krw, kb, cos_t, sin_t):
    b, r, d = xr.shape
    ts = ROW_TILE
    nt = r // ts
    const2 = lambda bb, i: (0, 0)
    const3 = lambda bb, i: (0, 0, 0)
    return pl.pallas_call(
        _mla_prep_kernel,
        grid=(b, nt),
        in_specs=[
            pl.BlockSpec((1, ts, d), lambda bb, i: (bb, i, 0)),
            _mod_spec(shift), _mod_spec(scale),
            pl.BlockSpec((1, d), const2),
            pl.BlockSpec(win.shape, const2),
            pl.BlockSpec(kvaw.shape, const2),
            pl.BlockSpec(qaw.shape, const2),
            pl.BlockSpec(wkvb_t.shape, const2),
            pl.BlockSpec(wqb_t.shape, const2),
            pl.BlockSpec(knw.shape, const3),
            pl.BlockSpec(qw.shape, const3),
            pl.BlockSpec(krw.shape, const3),
            pl.BlockSpec(kb.shape, const3),
            pl.BlockSpec((MLA_ROPE, ts), lambda bb, i: (0, i)),
            pl.BlockSpec((MLA_ROPE, ts), lambda bb, i: (0, i)),
        ],
        out_specs=[
            pl.BlockSpec((1, MLA_HEADS, MLA_QT_ROWS, ts), lambda bb, i: (bb, 0, 0, i)),
            pl.BlockSpec((1, MLA_HEADS, ts, K_LANES), lambda bb, i: (bb, 0, i, 0)),
            pl.BlockSpec((1, MLA_HEADS, 1, HEAD_V, ts), lambda bb, i: (bb, 0, i, 0, 0)),
            pl.BlockSpec((1, ts, MLA_WIDTH), lambda bb, i: (bb, i, 0)),
        ],
        out_shape=[
            jax.ShapeDtypeStruct((b, MLA_HEADS, MLA_QT_ROWS, r), _BF16),
            jax.ShapeDtypeStruct((b, MLA_HEADS, r, K_LANES), _BF16),
            jax.ShapeDtypeStruct((b, MLA_HEADS, nt, HEAD_V, ts), _BF16),
            jax.ShapeDtypeStruct((b, r, MLA_WIDTH), _BF16),
        ],
        compiler_params=_compiler_params(2),
        name="mla_prep",
    )(xr, shift, scale, nw, win, kvaw, qaw, wkvb_t, wqb_t, knw, qw, krw, kb, cos_t, sin_t)


def _attn_kernel(*refs, heads, chunks_per_head, n_lat_tiles, n_ctx_tiles):
    if n_lat_tiles:
        flag_ref, q_ref, kl_ref, vl_ref, kc_ref, vc_ref, o_ref, m_ref, l_ref, acc_ref = refs
    else:
        flag_ref, q_ref, kc_ref, vc_ref, o_ref, m_ref, l_ref, acc_ref = refs
    n_chunks = heads * chunks_per_head
    qk_rows = q_ref.shape[2]

    def group_sums(p):
        return jnp.sum(p.reshape(KV_TILE // F32_SUBLANES, F32_SUBLANES, Q_CHUNK), axis=0)

    def q_chunk(c):
        hh, cc = divmod(c, chunks_per_head)
        return hh, slice(cc * Q_CHUNK, (cc + 1) * Q_CHUNK)

    ctx_tiles = [(kc_ref, vc_ref, j) for j in range(n_ctx_tiles)]

    def scores(tile, c):
        k_ref, _, j = tile
        hh, cols = q_chunk(c)
        start = j * KV_TILE
        if not isinstance(j, int):
            start = pl.multiple_of(start, KV_TILE)
        kt = k_ref[0, 0, pl.ds(start, KV_TILE), :qk_rows]
        return jnp.dot(kt, q_ref[0, hh, :, cols], preferred_element_type=_F32)

    def pipelined(steps, consume):
        pending = [scores(tile, c) for tile, c in steps[:QK_AHEAD]]
        for t, (tile, c) in enumerate(steps):
            s = pending.pop(0)
            if t + QK_AHEAD < len(steps):
                pending.append(scores(*steps[t + QK_AHEAD]))
            consume(t, tile, c, s)

    def shift_softmax(tiles):
        part = {}

        def consume(t, tile, c, s):
            p = jnp.exp2(s)
            pv = jnp.dot(tile[1][0, 0, tile[2]], p.astype(_BF16), preferred_element_type=_F32)
            if tile is tiles[0]:
                part[c] = (pv, group_sums(p))
            else:
                part[c] = (part[c][0] + pv, part[c][1] + group_sums(p))
            if tile is tiles[-1]:
                acc_ref[c], l_ref[c] = part.pop(c)

        groups = [range(c0, c0 + CHUNKS_IN_FLIGHT) for c0 in range(0, n_chunks, CHUNKS_IN_FLIGHT)]
        pipelined([(tile, c) for group in groups for tile in tiles for c in group], consume)

    def online_softmax_update(tiles):
        def consume(t, tile, c, s):
            m_prev = m_ref[c]
            m_new = jnp.maximum(m_prev, jnp.max(s, axis=0, keepdims=True))
            alpha = jnp.exp2(m_prev - m_new)
            p = jnp.exp2(s - m_new)
            pv = jnp.dot(tile[1][0, 0, tile[2]], p.astype(_BF16), preferred_element_type=_F32)
            acc_ref[c] = alpha * acc_ref[c] + pv
            l_ref[c] = alpha * l_ref[c] + group_sums(p)
            m_ref[c] = m_new

        pipelined([(tile, c) for tile in tiles for c in range(n_chunks)], consume)

    shift_is_safe = flag_ref[0] != 0

    @pl.when(shift_is_safe)
    def _():
        lat_tiles = [(kl_ref, vl_ref, j) for j in range(n_lat_tiles)] if n_lat_tiles else []
        shift_softmax(lat_tiles + ctx_tiles)

    @pl.when(jnp.logical_not(shift_is_safe))
    def _():
        m_ref[...] = jnp.full(m_ref.shape, M_INIT, _F32)
        acc_ref[...] = jnp.zeros(acc_ref.shape, _F32)
        l_ref[...] = jnp.zeros(l_ref.shape, _F32)
        if n_lat_tiles:
            def body(jj, carry):
                online_softmax_update([(kl_ref, vl_ref, jj * ONLINE_TILES_PER_ITER + u)
                                       for u in range(ONLINE_TILES_PER_ITER)])
                return carry
            lax.fori_loop(0, n_lat_tiles // ONLINE_TILES_PER_ITER, body, 0)
        online_softmax_update(ctx_tiles)

    for c in range(n_chunks):
        hh, cols = q_chunk(c)
        denom = jnp.sum(l_ref[c], axis=0, keepdims=True)
        o_ref[0, hh, :, cols] = (acc_ref[c] / denom).astype(_BF16)


def _attention(shift_ok, qt, k_lat, v_lat, k_ctx, v_ctx, *, heads_per_step, tq, name):
    b, h, qk_rows, r = qt.shape
    kvh = k_ctx.shape[1]
    n_ctx_tiles = v_ctx.shape[2]
    n_lat_tiles = v_lat.shape[2] if v_lat is not None else 0
    chunks_per_head = tq // Q_CHUNK
    n_chunks = heads_per_step * chunks_per_head

    def kv_specs(k, v):
        return [pl.BlockSpec((1, 1) + k.shape[2:], lambda bb, g, i: (bb, g, 0, 0)),
                pl.BlockSpec((1, 1) + v.shape[2:], lambda bb, g, i: (bb, g, 0, 0, 0))]

    in_specs = [pl.BlockSpec(memory_space=pltpu.SMEM),
                pl.BlockSpec((1, heads_per_step, qk_rows, tq), lambda bb, g, i: (bb, g, 0, i))]
    args = [shift_ok, qt]
    if n_lat_tiles:
        in_specs += kv_specs(k_lat, v_lat)
        args += [k_lat, v_lat]
    in_specs += kv_specs(k_ctx, v_ctx)
    args += [k_ctx, v_ctx]
    return pl.pallas_call(
        functools.partial(_attn_kernel, heads=heads_per_step, chunks_per_head=chunks_per_head,
                          n_lat_tiles=n_lat_tiles, n_ctx_tiles=n_ctx_tiles),
        grid=(b, kvh, r // tq),
        in_specs=in_specs,
        out_specs=pl.BlockSpec((1, heads_per_step, HEAD_V, tq), lambda bb, g, i: (bb, g, 0, i)),
        out_shape=jax.ShapeDtypeStruct((b, h, HEAD_V, r), _BF16),
        scratch_shapes=[pltpu.VMEM((n_chunks, 1, Q_CHUNK), _F32),
                        pltpu.VMEM((n_chunks, F32_SUBLANES, Q_CHUNK), _F32),
                        pltpu.VMEM((n_chunks, HEAD_V, Q_CHUNK), _F32)],
        compiler_params=_compiler_params(3),
        name=name,
    )(*args)


def _outproj_kernel(o_ref, g_ref, x_ref, gate_ref, w_ref, y_ref):
    ts = x_ref.shape[1]
    o = o_ref[0].astype(_F32).reshape(o_ref.shape[1] * o_ref.shape[2], ts).T
    z = (o * g_ref[0].astype(_F32)).astype(_BF16)
    y = jnp.dot(z, w_ref[...], preferred_element_type=_F32)
    y_ref[0] = x_ref[0] + gate_ref[0] * y


def _outproj(ot, gs, xr, gate, w_out):
    b, r, d = xr.shape
    ts = ROW_TILE
    heads, hv = ot.shape[1], ot.shape[2]
    return pl.pallas_call(
        _outproj_kernel,
        grid=(b, r // ts),
        in_specs=[
            pl.BlockSpec((1, heads, hv, ts), lambda bb, i: (bb, 0, 0, i)),
            pl.BlockSpec((1, ts, heads * hv), lambda bb, i: (bb, i, 0)),
            pl.BlockSpec((1, ts, d), lambda bb, i: (bb, i, 0)),
            _mod_spec(gate),
            pl.BlockSpec(w_out.shape, lambda bb, i: (0, 0)),
        ],
        out_specs=pl.BlockSpec((1, ts, d), lambda bb, i: (bb, i, 0)),
        out_shape=jax.ShapeDtypeStruct((b, r, d), _F32),
        compiler_params=_compiler_params(2),
        name="outproj",
    )(ot, gs, xr, gate, w_out)


def _axial_rope_tables_t(n_tokens, rot_dim):
    rows = n_tokens // GRID_W
    row = jnp.repeat(jnp.arange(rows, dtype=_F32), GRID_W)
    col = jnp.tile(jnp.arange(GRID_W, dtype=_F32), rows)
    axis_dim = rot_dim // 2
    inv_freq = jnp.power(ROPE_THETA, -jnp.arange(0, axis_dim, 2, dtype=_F32) / axis_dim)
    ang_r = row[:, None] * inv_freq[None, :]
    ang_c = col[:, None] * inv_freq[None, :]
    ang = jnp.concatenate([ang_r, ang_r, ang_c, ang_c], axis=-1)
    return jnp.cos(ang).T, jnp.sin(ang).T


def _identity_rope_tables_t(n_tokens, rot_dim):
    return jnp.ones((rot_dim, n_tokens), _F32), jnp.zeros((rot_dim, n_tokens), _F32)


def _split_mod(mod, n_batch):
    d = D_MODEL
    lat = tuple(mod[:n_batch, None, k * d:(k + 1) * d] for k in range(3))
    ctx = tuple(mod[n_batch:n_batch + 1, None, k * d:(k + 1) * d] for k in range(3))
    return lat, ctx


def _shift_params(q_norm_bound, k_norm_bound):
    kb = k_norm_bound * SHIFT_MARGIN
    ok = (q_norm_bound * kb <= SHIFT_LIMIT).astype(jnp.int32)
    return kb.reshape(1, 1, 1), ok.reshape(1)


def kernel(x, c, ctx, c_ctx, l0_w_mod, l0_b_mod, l0_norm, l0_w_in, l0_q_norm, l0_k_norm, l0_w_out, l1_w_mod, l1_b_mod, l1_norm, l1_w_in, l1_kv_a_norm, l1_w_kv_b, l1_q_a_norm, l1_w_q_b, l1_q_norm, l1_k_nope_norm, l1_k_rope_norm, l1_w_out):
    n_batch, seq, d = x.shape
    ctx_len = ctx.shape[1]
    cond_rows = 8
    cond = jnp.concatenate([c, c_ctx[None, :], jnp.zeros((cond_rows - n_batch - 1, d), _F32)], axis=0)

    (sh_l, sc_l, gt_l), (sh_c, sc_c, gt_c) = _split_mod(_adaln(cond, l0_w_mod, l0_b_mod), n_batch)
    kw, kv_w = GQA_KV_WIDTH, 2 * GQA_KV_WIDTH
    w_k, w_v = l0_w_in[:, :kw], l0_w_in[:, kw:kv_w]
    w_q, w_g = l0_w_in[:, kv_w:kv_w + GQA_WIDTH], l0_w_in[:, kv_w + GQA_WIDTH:]
    wqkv_t = jnp.concatenate([w_q, w_k, w_v], axis=1).T.astype(_BF16)
    q_scale = GQA_HEAD_DIM ** -0.5 * LOG2E
    qkw = jnp.concatenate([jnp.tile((l0_q_norm * q_scale)[None], (GQA_HEADS, 1)),
                           jnp.tile(l0_k_norm[None], (GQA_KV_HEADS, 1))], axis=0)[:, :, None]
    kb0, ok0 = _shift_params(GQA_HEAD_DIM ** 0.5 * q_scale * jnp.max(jnp.abs(l0_q_norm)),
                             GQA_HEAD_DIM ** 0.5 * jnp.max(jnp.abs(l0_k_norm)))
    nw0 = l0_norm[None, :]
    wg0 = w_g.astype(_BF16)
    cos_a, sin_a = _axial_rope_tables_t(seq, GQA_HEAD_DIM)
    cos_i, sin_i = _identity_rope_tables_t(ctx_len, GQA_HEAD_DIM)
    q_l, k_l, v_l, g_l = _gqa_prep(x, sh_l, sc_l, nw0, wqkv_t, wg0, qkw, kb0, cos_a, sin_a)
    q_c, k_c, v_c, g_c = _gqa_prep(ctx, sh_c, sc_c, nw0, wqkv_t, wg0, qkw, kb0, cos_i, sin_i)
    o_l = _attention(ok0, q_l, k_l, v_l, k_c, v_c, heads_per_step=GQA_GROUP, tq=512, name="gqa_attn")
    o_c = _attention(ok0, q_c, None, None, k_c, v_c, heads_per_step=GQA_GROUP, tq=ctx_len, name="gqa_attn_ctx")
    w_out0 = l0_w_out.astype(_BF16)
    x = _outproj(o_l, g_l, x, gt_l, w_out0)
    ctx = _outproj(o_c, g_c, ctx, gt_c, w_out0)

    (sh_l, sc_l, gt_l), (sh_c, sc_c, _) = _split_mod(_adaln(cond, l1_w_mod, l1_b_mod), n_batch)
    c_kv, c_kr = MLA_KV_LORA, MLA_KV_LORA + MLA_ROPE
    c_q = c_kr + MLA_Q_LORA
    win = jnp.concatenate([l1_w_in[:, :c_kv], l1_w_in[:, c_kr:c_q], l1_w_in[:, c_q:], l1_w_in[:, c_kv:c_kr],
                           jnp.zeros((d, LANES - MLA_ROPE), _F32)], axis=1).astype(_BF16)
    wkvb = l1_w_kv_b.reshape(MLA_KV_LORA, MLA_HEADS, MLA_NOPE + MLA_V)
    wkvb_t = jnp.concatenate([wkvb[:, :, :MLA_NOPE].reshape(MLA_KV_LORA, -1),
                              wkvb[:, :, MLA_NOPE:].reshape(MLA_KV_LORA, -1)], axis=1).T.astype(_BF16)
    wqb_t = l1_w_q_b.T.astype(_BF16)
    q_scale = MLA_QK ** -0.5 * LOG2E
    nw1 = l1_norm[None, :]
    kvaw, qaw = l1_kv_a_norm[None, :], l1_q_a_norm[None, :]
    knw = l1_k_nope_norm[None, :, None]
    qw = (l1_q_norm * q_scale)[None, :, None]
    krw = l1_k_rope_norm[None, :, None]
    kb1, ok1 = _shift_params(MLA_QK ** 0.5 * q_scale * jnp.max(jnp.abs(l1_q_norm)),
                             jnp.sqrt(MLA_NOPE * jnp.max(jnp.square(l1_k_nope_norm))
                                      + MLA_ROPE * jnp.max(jnp.square(l1_k_rope_norm))))
    cos_b, sin_b = _axial_rope_tables_t(seq, MLA_ROPE)
    cos_i, sin_i = _identity_rope_tables_t(ctx_len, MLA_ROPE)
    q_l, k_l, v_l, g_l = _mla_prep(x, sh_l, sc_l, nw1, win, kvaw, qaw, wkvb_t, wqb_t, knw, qw, krw, kb1,
                                   cos_b, sin_b)
    _, k_c, v_c, _ = _mla_prep(ctx, sh_c, sc_c, nw1, win, kvaw, qaw, wkvb_t, wqb_t, knw, qw, krw, kb1,
                               cos_i, sin_i)
    o_l = _attention(ok1, q_l, k_l, v_l, k_c, v_c, heads_per_step=1, tq=2048, name="mla_attn")
    return _outproj(o_l, g_l, x, gt_l, l1_w_out.astype(_BF16))
```

```python
import functools
import math

import jax
import jax.numpy as jnp
from jax import lax
from jax.experimental import pallas as pl
from jax.experimental.pallas import tpu as pltpu

D_MODEL = 1024
GRID_W = 64
ROPE_THETA = 10000.0
NORM_EPS = 1e-6

GQA_HEADS = 16
GQA_KV_HEADS = 4
GQA_GROUP = GQA_HEADS // GQA_KV_HEADS
GQA_HEAD_DIM = 64
GQA_WIDTH = GQA_HEADS * GQA_HEAD_DIM
GQA_KV_WIDTH = GQA_KV_HEADS * GQA_HEAD_DIM

MLA_HEADS = 16
MLA_NOPE = 64
MLA_ROPE = 32
MLA_V = 64
MLA_Q_LORA = 384
MLA_KV_LORA = 256
MLA_QK = MLA_NOPE + MLA_ROPE
MLA_WIDTH = MLA_HEADS * MLA_V

LANES = 128
MXU_COLS = 256
BF16_SUBLANES = 16
VMEM_LIMIT_BYTES = 48 * 1024 * 1024

MAX_ROW_TILE = 512
FUSED_ROW_TILE = 256
KV_TILE = 256
Q_CHUNK = MXU_COLS
HEAD_V = 64
F32_SUBLANES = 8
K_LANES = LANES
GQA_QT_ROWS = GQA_HEAD_DIM + BF16_SUBLANES
MLA_QT_ROWS = K_LANES
QK_AHEAD = 4
CHUNKS_IN_FLIGHT = 2
ONLINE_TILES_PER_ITER = 4
LOG2E = math.log2(math.e)
M_INIT = -1e30
SHIFT_MARGIN = 1.0 + 2.0 ** -6
SHIFT_LIMIT = 60.0

_BF16 = jnp.bfloat16
_F32 = jnp.float32


def _silu(v):
    return v * jax.nn.sigmoid(v)


def _compiler_params(n_grid_axes, flags=None):
    return pltpu.CompilerParams(
        dimension_semantics=("arbitrary",) * n_grid_axes,
        vmem_limit_bytes=VMEM_LIMIT_BYTES,
        flags=flags,
    )


def _adaln_kernel(cond_ref, w_ref, b_ref, o_ref):
    a = _silu(cond_ref[...])
    o_ref[...] = jnp.dot(a, w_ref[...], preferred_element_type=_F32,
                         precision=lax.Precision.HIGHEST) + b_ref[...]


def _adaln(cond, w_mod, b_mod):
    rows, d = cond.shape
    n = w_mod.shape[1]
    tn = 1024
    return pl.pallas_call(
        _adaln_kernel,
        grid=(n // tn,),
        in_specs=[
            pl.BlockSpec((rows, d), lambda j: (0, 0)),
            pl.BlockSpec((d, tn), lambda j: (0, j)),
            pl.BlockSpec((1, tn), lambda j: (0, j)),
        ],
        out_specs=pl.BlockSpec((rows, tn), lambda j: (0, j)),
        out_shape=jax.ShapeDtypeStruct((rows, n), _F32),
        compiler_params=_compiler_params(1),
        name="adaln",
    )(cond, w_mod, b_mod.reshape(1, n))


def _modulated_norm(x, nw, scale, shift):
    ms = jnp.mean(x * x, axis=-1, keepdims=True)
    return (x * lax.rsqrt(ms + NORM_EPS) * nw) * (1.0 + scale) + shift


def _head_rms_norm_t(t, w):
    ms = jnp.mean(t * t, axis=1, keepdims=True)
    return t * lax.rsqrt(ms + NORM_EPS) * w


def _axial_rope_t(t, cos, sin):
    q = t.shape[1] // 4
    x1, x2, x3, x4 = t[:, 0:q], t[:, q:2 * q], t[:, 2 * q:3 * q], t[:, 3 * q:4 * q]
    rot = jnp.concatenate([-x2, x1, -x4, x3], axis=1)
    return t * cos[None] + rot * sin[None]


def _tail_rows(first_row, n_rows):
    heads, _, cols = first_row.shape
    row = lax.broadcasted_iota(jnp.int32, (heads, n_rows, cols), 1)
    return jnp.where(row == 0, jnp.broadcast_to(first_row, (heads, n_rows, cols)), 0.0)


def _store_v_tiles(v_ref, vt):
    for u in range(vt.shape[2] // KV_TILE):
        v_ref[0, :, u] = vt[:, :, u * KV_TILE:(u + 1) * KV_TILE].astype(_BF16)


def _row_tile(rows):
    return min(MAX_ROW_TILE, rows)


def _neg_shift_row(q, kb):
    return -(jnp.sqrt(jnp.sum(q * q, axis=1, keepdims=True)) * kb)


def _gqa_prep_kernel(x_ref, shift_ref, scale_ref, nw_ref, wqkv_ref, wg_ref, qkw_ref, kb_ref, cos_ref, sin_ref,
                     q_ref, k_ref, v_ref, g_ref):
    ts = x_ref.shape[1]
    h = _modulated_norm(x_ref[0], nw_ref[...], scale_ref[0], shift_ref[0])
    g = jnp.dot(h.astype(_BF16), wg_ref[...], preferred_element_type=_F32)
    g_ref[0] = _silu(g).astype(_BF16)
    ht = h.T.astype(_BF16)
    a = jnp.dot(wqkv_ref[...], ht, preferred_element_type=_F32)
    n_qk = GQA_HEADS + GQA_KV_HEADS
    qk = a[:n_qk * GQA_HEAD_DIM].reshape(n_qk, GQA_HEAD_DIM, ts)
    qk = _head_rms_norm_t(qk, qkw_ref[...])
    qk = _axial_rope_t(qk, cos_ref[...], sin_ref[...])
    q = qk[:GQA_HEADS]
    q_tail = _tail_rows(_neg_shift_row(q, kb_ref[...]), GQA_QT_ROWS - GQA_HEAD_DIM)
    q_ref[0] = jnp.concatenate([q, q_tail], axis=1).astype(_BF16)
    k_tail = _tail_rows(jnp.ones((GQA_KV_HEADS, 1, ts), _F32), K_LANES - GQA_HEAD_DIM)
    kn = jnp.concatenate([qk[GQA_HEADS:], k_tail], axis=1).reshape(GQA_KV_HEADS * K_LANES, ts).T
    for hh in range(GQA_KV_HEADS):
        k_ref[0, hh] = kn[:, hh * K_LANES:(hh + 1) * K_LANES].astype(_BF16)
    _store_v_tiles(v_ref, a[n_qk * GQA_HEAD_DIM:].reshape(GQA_KV_HEADS, GQA_HEAD_DIM, ts))


def _mod_spec(arr):
    per_batch = arr.shape[0] > 1
    return pl.BlockSpec((1, 1, D_MODEL), lambda b, i: (b if per_batch else 0, 0, 0))


def _gqa_prep(xr, shift, scale, nw, wqkv_t, wg, qkw, kb, cos_t, sin_t):
    b, r, d = xr.shape
    ts = _row_tile(r)
    return pl.pallas_call(
        _gqa_prep_kernel,
        grid=(b, r // ts),
        in_specs=[
            pl.BlockSpec((1, ts, d), lambda bb, i: (bb, i, 0)),
            _mod_spec(shift), _mod_spec(scale),
            pl.BlockSpec((1, d), lambda bb, i: (0, 0)),
            pl.BlockSpec(wqkv_t.shape, lambda bb, i: (0, 0)),
            pl.BlockSpec(wg.shape, lambda bb, i: (0, 0)),
            pl.BlockSpec(qkw.shape, lambda bb, i: (0, 0, 0)),
            pl.BlockSpec(kb.shape, lambda bb, i: (0, 0, 0)),
            pl.BlockSpec((GQA_HEAD_DIM, ts), lambda bb, i: (0, i)),
            pl.BlockSpec((GQA_HEAD_DIM, ts), lambda bb, i: (0, i)),
        ],
        out_specs=[
            pl.BlockSpec((1, GQA_HEADS, GQA_QT_ROWS, ts), lambda bb, i: (bb, 0, 0, i)),
            pl.BlockSpec((1, GQA_KV_HEADS, ts, K_LANES), lambda bb, i: (bb, 0, i, 0)),
            pl.BlockSpec((1, GQA_KV_HEADS, ts // KV_TILE, HEAD_V, KV_TILE), lambda bb, i: (bb, 0, i, 0, 0)),
            pl.BlockSpec((1, ts, GQA_WIDTH), lambda bb, i: (bb, i, 0)),
        ],
        out_shape=[
            jax.ShapeDtypeStruct((b, GQA_HEADS, GQA_QT_ROWS, r), _BF16),
            jax.ShapeDtypeStruct((b, GQA_KV_HEADS, r, K_LANES), _BF16),
            jax.ShapeDtypeStruct((b, GQA_KV_HEADS, r // KV_TILE, HEAD_V, KV_TILE), _BF16),
            jax.ShapeDtypeStruct((b, r, GQA_WIDTH), _BF16),
        ],
        compiler_params=_compiler_params(2),
        name="gqa_prep",
    )(xr, shift, scale, nw, wqkv_t, wg, qkw, kb, cos_t, sin_t)


def _mla_prep_kernel(x_ref, *refs):
    _mla_prep_rows(x_ref[0], *refs)


def _mla_prep_rows(x, shift_ref, scale_ref, nw_ref, win_ref, kvaw_ref, qaw_ref, wkvb_ref, wqb_ref,
                   knw_ref, qw_ref, krw_ref, kb_ref, cos_ref, sin_ref, q_ref, k_ref, v_ref, g_ref):
    ts = x.shape[0]
    h = _modulated_norm(x, nw_ref[...], scale_ref[0], shift_ref[0])
    p = jnp.dot(h.astype(_BF16), win_ref[...], preferred_element_type=_F32)
    c0, c1, c2 = MLA_KV_LORA, MLA_KV_LORA + MLA_Q_LORA, MLA_KV_LORA + MLA_Q_LORA + MLA_WIDTH
    kv_a, q_a, g, kr = p[:, :c0], p[:, c0:c1], p[:, c1:c2], p[:, c2:]
    g_ref[0] = _silu(g).astype(_BF16)

    def row_norm(t, w):
        return t * lax.rsqrt(jnp.mean(t * t, axis=-1, keepdims=True) + NORM_EPS) * w

    kv_an = row_norm(kv_a, kvaw_ref[...])
    q_an = row_norm(q_a, qaw_ref[...])
    kvt = jnp.dot(wkvb_ref[...], kv_an.T.astype(_BF16), preferred_element_type=_F32)
    kn = _head_rms_norm_t(kvt[:MLA_HEADS * MLA_NOPE].reshape(MLA_HEADS, MLA_NOPE, ts), knw_ref[...])
    vt = kvt[MLA_HEADS * MLA_NOPE:].reshape(MLA_HEADS, MLA_V, ts)
    ones_row = jnp.ones((MLA_HEADS, 1, ts), _F32)
    _store_v_tiles(v_ref, vt)

    qt = jnp.dot(wqb_ref[...], q_an.T.astype(_BF16), preferred_element_type=_F32)
    qt = _head_rms_norm_t(qt.reshape(MLA_HEADS, MLA_QK, ts), qw_ref[...])
    q_nope = qt[:, :MLA_NOPE]
    q_rope = _axial_rope_t(qt[:, MLA_NOPE:], cos_ref[...], sin_ref[...])
    neg_b = -(jnp.sqrt(jnp.sum(q_nope * q_nope, axis=1, keepdims=True)
                       + jnp.sum(q_rope * q_rope, axis=1, keepdims=True)) * kb_ref[...])
    q_ref[0] = jnp.concatenate([q_nope, q_rope, _tail_rows(neg_b, MLA_QT_ROWS - MLA_QK)], axis=1).astype(_BF16)

    krt = kr.T[:MLA_ROPE].reshape(1, MLA_ROPE, ts)
    krt = _axial_rope_t(_head_rms_norm_t(krt, krw_ref[...]), cos_ref[...], sin_ref[...])
    kft = jnp.concatenate([kn, jnp.broadcast_to(krt, (MLA_HEADS, MLA_ROPE, ts)),
                           _tail_rows(ones_row, K_LANES - MLA_QK)], axis=1)
    kf = kft.reshape(MLA_HEADS * K_LANES, ts).T
    for hh in range(MLA_HEADS):
        k_ref[0, hh] = kf[:, hh * K_LANES:(hh + 1) * K_LANES].astype(_BF16)


def _mla_prep_specs(b, r, ts, params):
    shift, scale, nw, win, kvaw, qaw, wkvb_t, wqb_t, knw, qw, krw, kb, cos_t, sin_t = params
    const2 = lambda bb, i: (0, 0)
    const3 = lambda bb, i: (0, 0, 0)
    in_specs = [_mod_spec(shift), _mod_spec(scale)]
    in_specs += [pl.BlockSpec(a.shape, const2) for a in (nw, win, kvaw, qaw, wkvb_t, wqb_t)]
    in_specs += [pl.BlockSpec(a.shape, const3) for a in (knw, qw, krw, kb)]
    in_specs += [pl.BlockSpec((MLA_ROPE, ts), lambda bb, i: (0, i))] * 2
    out_specs = [
        pl.BlockSpec((1, MLA_HEADS, MLA_QT_ROWS, ts), lambda bb, i: (bb, 0, 0, i)),
        pl.BlockSpec((1, MLA_HEADS, ts, K_LANES), lambda bb, i: (bb, 0, i, 0)),
        pl.BlockSpec((1, MLA_HEADS, ts // KV_TILE, HEAD_V, KV_TILE), lambda bb, i: (bb, 0, i, 0, 0)),
        pl.BlockSpec((1, ts, MLA_WIDTH), lambda bb, i: (bb, i, 0)),
    ]
    out_shape = [
        jax.ShapeDtypeStruct((b, MLA_HEADS, MLA_QT_ROWS, r), _BF16),
        jax.ShapeDtypeStruct((b, MLA_HEADS, r, K_LANES), _BF16),
        jax.ShapeDtypeStruct((b, MLA_HEADS, r // KV_TILE, HEAD_V, KV_TILE), _BF16),
        jax.ShapeDtypeStruct((b, r, MLA_WIDTH), _BF16),
    ]
    return in_specs, out_specs, out_shape


def _mla_prep(xr, *params):
    b, r, d = xr.shape
    ts = _row_tile(r)
    in_specs, out_specs, out_shape = _mla_prep_specs(b, r, ts, params)
    return pl.pallas_call(
        _mla_prep_kernel,
        grid=(b, r // ts),
        in_specs=[pl.BlockSpec((1, ts, d), lambda bb, i: (bb, i, 0))] + in_specs,
        out_specs=out_specs,
        out_shape=out_shape,
        compiler_params=_compiler_params(2),
        name="mla_prep",
    )(xr, *params)


def _attn_kernel(*refs, heads, chunks_per_head, n_lat_tiles, n_ctx_tiles):
    if n_lat_tiles:
        flag_ref, q_ref, kl_ref, vl_ref, kc_ref, vc_ref, o_ref, m_ref, l_ref, acc_ref = refs
    else:
        flag_ref, q_ref, kc_ref, vc_ref, o_ref, m_ref, l_ref, acc_ref = refs
    n_chunks = heads * chunks_per_head
    qk_rows = q_ref.shape[2]

    def group_sums(p):
        return jnp.sum(p.reshape(KV_TILE // F32_SUBLANES, F32_SUBLANES, Q_CHUNK), axis=0)

    def q_chunk(c):
        hh, cc = divmod(c, chunks_per_head)
        return hh, slice(cc * Q_CHUNK, (cc + 1) * Q_CHUNK)

    ctx_tiles = [(kc_ref, vc_ref, j) for j in range(n_ctx_tiles)]

    def scores(tile, c):
        k_ref, _, j = tile
        hh, cols = q_chunk(c)
        start = j * KV_TILE
        if not isinstance(j, int):
            start = pl.multiple_of(start, KV_TILE)
        kt = k_ref[0, 0, pl.ds(start, KV_TILE), :qk_rows]
        return jnp.dot(kt, q_ref[0, hh, :, cols], preferred_element_type=_F32)

    def pipelined(steps, consume):
        pending = [scores(tile, c) for tile, c in steps[:QK_AHEAD]]
        for t, (tile, c) in enumerate(steps):
            s = pending.pop(0)
            if t + QK_AHEAD < len(steps):
                pending.append(scores(*steps[t + QK_AHEAD]))
            consume(t, tile, c, s)

    def shift_softmax(tiles):
        part = {}

        def consume(t, tile, c, s):
            p = jnp.exp2(s)
            pv = jnp.dot(tile[1][0, 0, tile[2]], p.astype(_BF16), preferred_element_type=_F32)
            if tile is tiles[0]:
                part[c] = (pv, group_sums(p))
            else:
                part[c] = (part[c][0] + pv, part[c][1] + group_sums(p))
            if tile is tiles[-1]:
                acc_ref[c], l_ref[c] = part.pop(c)

        groups = [range(c0, c0 + CHUNKS_IN_FLIGHT) for c0 in range(0, n_chunks, CHUNKS_IN_FLIGHT)]
        pipelined([(tile, c) for group in groups for tile in tiles for c in group], consume)

    def online_softmax_update(tiles):
        def consume(t, tile, c, s):
            m_prev = m_ref[c]
            m_new = jnp.maximum(m_prev, jnp.max(s, axis=0, keepdims=True))
            alpha = jnp.exp2(m_prev - m_new)
            p = jnp.exp2(s - m_new)
            pv = jnp.dot(tile[1][0, 0, tile[2]], p.astype(_BF16), preferred_element_type=_F32)
            acc_ref[c] = alpha * acc_ref[c] + pv
            l_ref[c] = alpha * l_ref[c] + group_sums(p)
            m_ref[c] = m_new

        pipelined([(tile, c) for tile in tiles for c in range(n_chunks)], consume)

    shift_is_safe = flag_ref[0] != 0

    @pl.when(shift_is_safe)
    def _():
        lat_tiles = [(kl_ref, vl_ref, j) for j in range(n_lat_tiles)] if n_lat_tiles else []
        shift_softmax(lat_tiles + ctx_tiles)

    @pl.when(jnp.logical_not(shift_is_safe))
    def _():
        m_ref[...] = jnp.full(m_ref.shape, M_INIT, _F32)
        acc_ref[...] = jnp.zeros(acc_ref.shape, _F32)
        l_ref[...] = jnp.zeros(l_ref.shape, _F32)
        if n_lat_tiles:
            def body(jj, carry):
                online_softmax_update([(kl_ref, vl_ref, jj * ONLINE_TILES_PER_ITER + u)
                                       for u in range(ONLINE_TILES_PER_ITER)])
                return carry
            lax.fori_loop(0, n_lat_tiles // ONLINE_TILES_PER_ITER, body, 0)
        online_softmax_update(ctx_tiles)

    for c in range(n_chunks):
        hh, cols = q_chunk(c)
        denom = jnp.sum(l_ref[c], axis=0, keepdims=True)
        o_ref[0, hh, :, cols] = (acc_ref[c] / denom).astype(_BF16)


def _attention(shift_ok, qt, k_lat, v_lat, k_ctx, v_ctx, *, heads_per_step, tq, name):
    b, h, qk_rows, r = qt.shape
    kvh = k_ctx.shape[1]
    n_ctx_tiles = v_ctx.shape[2]
    n_lat_tiles = v_lat.shape[2] if v_lat is not None else 0
    chunks_per_head = tq // Q_CHUNK
    n_chunks = heads_per_step * chunks_per_head

    def kv_specs(k, v):
        return [pl.BlockSpec((1, 1) + k.shape[2:], lambda bb, g, i: (bb, g, 0, 0)),
                pl.BlockSpec((1, 1) + v.shape[2:], lambda bb, g, i: (bb, g, 0, 0, 0))]

    in_specs = [pl.BlockSpec(memory_space=pltpu.SMEM),
                pl.BlockSpec((1, heads_per_step, qk_rows, tq), lambda bb, g, i: (bb, g, 0, i))]
    args = [shift_ok, qt]
    if n_lat_tiles:
        in_specs += kv_specs(k_lat, v_lat)
        args += [k_lat, v_lat]
    in_specs += kv_specs(k_ctx, v_ctx)
    args += [k_ctx, v_ctx]
    return pl.pallas_call(
        functools.partial(_attn_kernel, heads=heads_per_step, chunks_per_head=chunks_per_head,
                          n_lat_tiles=n_lat_tiles, n_ctx_tiles=n_ctx_tiles),
        grid=(b, kvh, r // tq),
        in_specs=in_specs,
        out_specs=pl.BlockSpec((1, heads_per_step, HEAD_V, tq), lambda bb, g, i: (bb, g, 0, i)),
        out_shape=jax.ShapeDtypeStruct((b, h, HEAD_V, r), _BF16),
        scratch_shapes=[pltpu.VMEM((n_chunks, 1, Q_CHUNK), _F32),
                        pltpu.VMEM((n_chunks, F32_SUBLANES, Q_CHUNK), _F32),
                        pltpu.VMEM((n_chunks, HEAD_V, Q_CHUNK), _F32)],
        compiler_params=_compiler_params(3),
        name=name,
    )(*args)


def _outproj_rows(o_ref, g_ref, x_ref, gate_ref, w_ref):
    ts = x_ref.shape[1]
    o = o_ref[0].astype(_F32).reshape(o_ref.shape[1] * o_ref.shape[2], ts).T
    z = (o * g_ref[0].astype(_F32)).astype(_BF16)
    y = jnp.dot(z, w_ref[...], preferred_element_type=_F32)
    return x_ref[0] + gate_ref[0] * y


def _outproj_kernel(o_ref, g_ref, x_ref, gate_ref, w_ref, y_ref):
    y_ref[0] = _outproj_rows(o_ref, g_ref, x_ref, gate_ref, w_ref)


def _outproj_specs(ot, xr, gate, w_out, ts):
    b, r, d = xr.shape
    heads, hv = ot.shape[1], ot.shape[2]
    in_specs = [
        pl.BlockSpec((1, heads, hv, ts), lambda bb, i: (bb, 0, 0, i)),
        pl.BlockSpec((1, ts, heads * hv), lambda bb, i: (bb, i, 0)),
        pl.BlockSpec((1, ts, d), lambda bb, i: (bb, i, 0)),
        _mod_spec(gate),
        pl.BlockSpec(w_out.shape, lambda bb, i: (0, 0)),
    ]
    return in_specs, pl.BlockSpec((1, ts, d), lambda bb, i: (bb, i, 0)), jax.ShapeDtypeStruct((b, r, d), _F32)


def _outproj(ot, gs, xr, gate, w_out):
    b, r, _ = xr.shape
    ts = _row_tile(r)
    in_specs, out_spec, out_shape = _outproj_specs(ot, xr, gate, w_out, ts)
    return pl.pallas_call(
        _outproj_kernel,
        grid=(b, r // ts),
        in_specs=in_specs,
        out_specs=out_spec,
        out_shape=out_shape,
        compiler_params=_compiler_params(2),
        name="outproj",
    )(ot, gs, xr, gate, w_out)


def _outproj_mla_prep_kernel(o_ref, g0_ref, x_ref, gate_ref, w_ref, *refs):
    prep_in, (y_ref, *prep_out) = refs[:-5], refs[-5:]
    y = _outproj_rows(o_ref, g0_ref, x_ref, gate_ref, w_ref)
    y_ref[0] = y
    _mla_prep_rows(y, *prep_in, *prep_out)


def _outproj_mla_prep(ot, gs, xr, gate, w_out, *prep_params):
    b, r, _ = xr.shape
    ts = min(FUSED_ROW_TILE, r)
    o_in, y_spec, y_shape = _outproj_specs(ot, xr, gate, w_out, ts)
    p_in, p_out_specs, p_out_shape = _mla_prep_specs(b, r, ts, prep_params)
    return pl.pallas_call(
        _outproj_mla_prep_kernel,
        grid=(b, r // ts),
        in_specs=o_in + p_in,
        out_specs=[y_spec] + p_out_specs,
        out_shape=[y_shape] + p_out_shape,
        compiler_params=_compiler_params(2),
        name="outproj_mla_prep",
    )(ot, gs, xr, gate, w_out, *prep_params)


def _axial_rope_tables_t(n_tokens, rot_dim):
    rows = n_tokens // GRID_W
    row = jnp.repeat(jnp.arange(rows, dtype=_F32), GRID_W)
    col = jnp.tile(jnp.arange(GRID_W, dtype=_F32), rows)
    axis_dim = rot_dim // 2
    inv_freq = jnp.power(ROPE_THETA, -jnp.arange(0, axis_dim, 2, dtype=_F32) / axis_dim)
    ang_r = row[:, None] * inv_freq[None, :]
    ang_c = col[:, None] * inv_freq[None, :]
    ang = jnp.concatenate([ang_r, ang_r, ang_c, ang_c], axis=-1)
    return jnp.cos(ang).T, jnp.sin(ang).T


def _identity_rope_tables_t(n_tokens, rot_dim):
    return jnp.ones((rot_dim, n_tokens), _F32), jnp.zeros((rot_dim, n_tokens), _F32)


def _split_mod(mod, n_batch):
    d = D_MODEL
    lat = tuple(mod[:n_batch, None, k * d:(k + 1) * d] for k in range(3))
    ctx = tuple(mod[n_batch:n_batch + 1, None, k * d:(k + 1) * d] for k in range(3))
    return lat, ctx


def _shift_params(q_norm_bound, k_norm_bound):
    kb = k_norm_bound * SHIFT_MARGIN
    ok = (q_norm_bound * kb <= SHIFT_LIMIT).astype(jnp.int32)
    return kb.reshape(1, 1, 1), ok.reshape(1)


def kernel(x, c, ctx, c_ctx, l0_w_mod, l0_b_mod, l0_norm, l0_w_in, l0_q_norm, l0_k_norm, l0_w_out, l1_w_mod, l1_b_mod, l1_norm, l1_w_in, l1_kv_a_norm, l1_w_kv_b, l1_q_a_norm, l1_w_q_b, l1_q_norm, l1_k_nope_norm, l1_k_rope_norm, l1_w_out):
    n_batch, seq, d = x.shape
    ctx_len = ctx.shape[1]
    cond_rows = 8
    cond = jnp.concatenate([c, c_ctx[None, :], jnp.zeros((cond_rows - n_batch - 1, d), _F32)], axis=0)

    (sh_l, sc_l, gt_l), (sh_c, sc_c, gt_c) = _split_mod(_adaln(cond, l0_w_mod, l0_b_mod), n_batch)
    kw, kv_w = GQA_KV_WIDTH, 2 * GQA_KV_WIDTH
    w_k, w_v = l0_w_in[:, :kw], l0_w_in[:, kw:kv_w]
    w_q, w_g = l0_w_in[:, kv_w:kv_w + GQA_WIDTH], l0_w_in[:, kv_w + GQA_WIDTH:]
    wqkv_t = jnp.concatenate([w_q, w_k, w_v], axis=1).T.astype(_BF16)
    q_scale = GQA_HEAD_DIM ** -0.5 * LOG2E
    qkw = jnp.concatenate([jnp.tile((l0_q_norm * q_scale)[None], (GQA_HEADS, 1)),
                           jnp.tile(l0_k_norm[None], (GQA_KV_HEADS, 1))], axis=0)[:, :, None]
    kb0, ok0 = _shift_params(GQA_HEAD_DIM ** 0.5 * q_scale * jnp.max(jnp.abs(l0_q_norm)),
                             GQA_HEAD_DIM ** 0.5 * jnp.max(jnp.abs(l0_k_norm)))
    nw0 = l0_norm[None, :]
    wg0 = w_g.astype(_BF16)
    cos_a, sin_a = _axial_rope_tables_t(seq, GQA_HEAD_DIM)
    cos_i, sin_i = _identity_rope_tables_t(ctx_len, GQA_HEAD_DIM)
    q_l, k_l, v_l, g_l = _gqa_prep(x, sh_l, sc_l, nw0, wqkv_t, wg0, qkw, kb0, cos_a, sin_a)
    q_c, k_c, v_c, g_c = _gqa_prep(ctx, sh_c, sc_c, nw0, wqkv_t, wg0, qkw, kb0, cos_i, sin_i)
    o_l = _attention(ok0, q_l, k_l, v_l, k_c, v_c, heads_per_step=GQA_GROUP, tq=512, name="gqa_attn")
    o_c = _attention(ok0, q_c, None, None, k_c, v_c, heads_per_step=GQA_GROUP, tq=ctx_len, name="gqa_attn_ctx")
    w_out0 = l0_w_out.astype(_BF16)
    ctx = _outproj(o_c, g_c, ctx, gt_c, w_out0)

    gate0_l = gt_l
    (sh_l, sc_l, gt_l), (sh_c, sc_c, _) = _split_mod(_adaln(cond, l1_w_mod, l1_b_mod), n_batch)
    c_kv, c_kr = MLA_KV_LORA, MLA_KV_LORA + MLA_ROPE
    c_q = c_kr + MLA_Q_LORA
    win = jnp.concatenate([l1_w_in[:, :c_kv], l1_w_in[:, c_kr:c_q], l1_w_in[:, c_q:], l1_w_in[:, c_kv:c_kr],
                           jnp.zeros((d, LANES - MLA_ROPE), _F32)], axis=1).astype(_BF16)
    wkvb = l1_w_kv_b.reshape(MLA_KV_LORA, MLA_HEADS, MLA_NOPE + MLA_V)
    wkvb_t = jnp.concatenate([wkvb[:, :, :MLA_NOPE].reshape(MLA_KV_LORA, -1),
                              wkvb[:, :, MLA_NOPE:].reshape(MLA_KV_LORA, -1)], axis=1).T.astype(_BF16)
    wqb_t = l1_w_q_b.T.astype(_BF16)
    q_scale = MLA_QK ** -0.5 * LOG2E
    nw1 = l1_norm[None, :]
    kvaw, qaw = l1_kv_a_norm[None, :], l1_q_a_norm[None, :]
    knw = l1_k_nope_norm[None, :, None]
    qw = (l1_q_norm * q_scale)[None, :, None]
    krw = l1_k_rope_norm[None, :, None]
    kb1, ok1 = _shift_params(MLA_QK ** 0.5 * q_scale * jnp.max(jnp.abs(l1_q_norm)),
                             jnp.sqrt(MLA_NOPE * jnp.max(jnp.square(l1_k_nope_norm))
                                      + MLA_ROPE * jnp.max(jnp.square(l1_k_rope_norm))))
    cos_b, sin_b = _axial_rope_tables_t(seq, MLA_ROPE)
    cos_i, sin_i = _identity_rope_tables_t(ctx_len, MLA_ROPE)
    x, q_l, k_l, v_l, g_l = _outproj_mla_prep(o_l, g_l, x, gate0_l, w_out0, sh_l, sc_l, nw1, win, kvaw, qaw,
                                              wkvb_t, wqb_t, knw, qw, krw, kb1, cos_b, sin_b)
    _, k_c, v_c, _ = _mla_prep(ctx, sh_c, sc_c, nw1, win, kvaw, qaw, wkvb_t, wqb_t, knw, qw, krw, kb1,
                               cos_i, sin_i)
    o_l = _attention(ok1, q_l, k_l, v_l, k_c, v_c, heads_per_step=1, tq=2048, name="mla_attn")
    return _outproj(o_l, g_l, x, gt_l, l1_w_out.astype(_BF16))
```

```python
import functools
import math

import jax
import jax.numpy as jnp
from jax import lax
from jax.experimental import pallas as pl
from jax.experimental.pallas import tpu as pltpu

D_MODEL = 1024
GRID_W = 64
ROPE_THETA = 10000.0
NORM_EPS = 1e-6

GQA_HEADS = 16
GQA_KV_HEADS = 4
GQA_GROUP = GQA_HEADS // GQA_KV_HEADS
GQA_HEAD_DIM = 64
GQA_WIDTH = GQA_HEADS * GQA_HEAD_DIM
GQA_KV_WIDTH = GQA_KV_HEADS * GQA_HEAD_DIM

MLA_HEADS = 16
MLA_NOPE = 64
MLA_ROPE = 32
MLA_V = 64
MLA_Q_LORA = 384
MLA_KV_LORA = 256
MLA_QK = MLA_NOPE + MLA_ROPE
MLA_WIDTH = MLA_HEADS * MLA_V

LANES = 128
MXU_COLS = 256
BF16_SUBLANES = 16
VMEM_LIMIT_BYTES = 48 * 1024 * 1024

MAX_ROW_TILE = 512
FUSED_ROW_TILE = 256
KV_TILE = 256
Q_CHUNK = MXU_COLS
HEAD_V = 64
F32_SUBLANES = 8
K_LANES = LANES
GQA_QT_ROWS = GQA_HEAD_DIM + BF16_SUBLANES
MLA_QT_ROWS = K_LANES
QK_AHEAD = 4
CHUNKS_IN_FLIGHT = 2
ONLINE_TILES_PER_ITER = 4
LOG2E = math.log2(math.e)
M_INIT = -1e30
SHIFT_MARGIN = 1.0 + 2.0 ** -6
SHIFT_LIMIT = 60.0

_BF16 = jnp.bfloat16
_F32 = jnp.float32


def _silu(v):
    return v * jax.nn.sigmoid(v)


def _compiler_params(n_grid_axes, flags=None):
    return pltpu.CompilerParams(
        dimension_semantics=("arbitrary",) * n_grid_axes,
        vmem_limit_bytes=VMEM_LIMIT_BYTES,
        flags=flags,
    )


def _adaln_kernel(cond_ref, w_ref, b_ref, o_ref):
    a = _silu(cond_ref[...])
    o_ref[...] = jnp.dot(a, w_ref[...], preferred_element_type=_F32,
                         precision=lax.Precision.HIGHEST) + b_ref[...]


def _adaln(cond, w_mod, b_mod):
    rows, d = cond.shape
    n = w_mod.shape[1]
    tn = 1024
    return pl.pallas_call(
        _adaln_kernel,
        grid=(n // tn,),
        in_specs=[
            pl.BlockSpec((rows, d), lambda j: (0, 0)),
            pl.BlockSpec((d, tn), lambda j: (0, j)),
            pl.BlockSpec((1, tn), lambda j: (0, j)),
        ],
        out_specs=pl.BlockSpec((rows, tn), lambda j: (0, j)),
        out_shape=jax.ShapeDtypeStruct((rows, n), _F32),
        compiler_params=_compiler_params(1),
        name="adaln",
    )(cond, w_mod, b_mod.reshape(1, n))


def _modulated_norm(x, nw, scale, shift):
    ms = jnp.mean(x * x, axis=-1, keepdims=True)
    return (x * lax.rsqrt(ms + NORM_EPS) * nw) * (1.0 + scale) + shift


def _head_rms_norm_t(t, w):
    ms = jnp.mean(t * t, axis=1, keepdims=True)
    return t * lax.rsqrt(ms + NORM_EPS) * w


def _axial_rope_t(t, cos, sin):
    q = t.shape[1] // 4
    x1, x2, x3, x4 = t[:, 0:q], t[:, q:2 * q], t[:, 2 * q:3 * q], t[:, 3 * q:4 * q]
    rot = jnp.concatenate([-x2, x1, -x4, x3], axis=1)
    return t * cos[None] + rot * sin[None]


def _tail_rows(first_row, n_rows):
    heads, _, cols = first_row.shape
    row = lax.broadcasted_iota(jnp.int32, (heads, n_rows, cols), 1)
    return jnp.where(row == 0, jnp.broadcast_to(first_row, (heads, n_rows, cols)), 0.0)


def _row_tile(rows):
    return min(MAX_ROW_TILE, rows)


def _neg_shift_row(q, kb):
    return -(jnp.sqrt(jnp.sum(q * q, axis=1, keepdims=True)) * kb)


def _gqa_prep_kernel(x_ref, shift_ref, scale_ref, nw_ref, wqkv_ref, wg_ref, qkw_ref, kb_ref, cos_ref, sin_ref,
                     q_ref, k_ref, v_ref, g_ref):
    ts = KV_TILE
    for u in range(x_ref.shape[1] // ts):
        rows = slice(u * ts, (u + 1) * ts)
        h = _modulated_norm(x_ref[0, rows], nw_ref[...], scale_ref[0], shift_ref[0])
        g = jnp.dot(h.astype(_BF16), wg_ref[...], preferred_element_type=_F32)
        g_ref[0, rows] = _silu(g).astype(_BF16)
        ht = h.T.astype(_BF16)
        a = jnp.dot(wqkv_ref[...], ht, preferred_element_type=_F32)
        n_qk = GQA_HEADS + GQA_KV_HEADS
        qk = a[:n_qk * GQA_HEAD_DIM].reshape(n_qk, GQA_HEAD_DIM, ts)
        qk = _head_rms_norm_t(qk, qkw_ref[...])
        qk = _axial_rope_t(qk, cos_ref[:, rows], sin_ref[:, rows])
        q = qk[:GQA_HEADS]
        q_tail = _tail_rows(_neg_shift_row(q, kb_ref[...]), GQA_QT_ROWS - GQA_HEAD_DIM)
        q_ref[0, :, :, rows] = jnp.concatenate([q, q_tail], axis=1).astype(_BF16)
        k_tail = _tail_rows(jnp.ones((GQA_KV_HEADS, 1, ts), _F32), K_LANES - GQA_HEAD_DIM)
        kn = jnp.concatenate([qk[GQA_HEADS:], k_tail], axis=1).reshape(GQA_KV_HEADS * K_LANES, ts).T
        for hh in range(GQA_KV_HEADS):
            k_ref[0, hh, rows] = kn[:, hh * K_LANES:(hh + 1) * K_LANES].astype(_BF16)
        v_ref[0, :, u] = a[n_qk * GQA_HEAD_DIM:].reshape(GQA_KV_HEADS, GQA_HEAD_DIM, ts).astype(_BF16)


def _mod_spec(arr):
    per_batch = arr.shape[0] > 1
    return pl.BlockSpec((1, 1, D_MODEL), lambda b, i: (b if per_batch else 0, 0, 0))


def _gqa_prep(xr, shift, scale, nw, wqkv_t, wg, qkw, kb, cos_t, sin_t):
    b, r, d = xr.shape
    ts = _row_tile(r)
    return pl.pallas_call(
        _gqa_prep_kernel,
        grid=(b, r // ts),
        in_specs=[
            pl.BlockSpec((1, ts, d), lambda bb, i: (bb, i, 0)),
            _mod_spec(shift), _mod_spec(scale),
            pl.BlockSpec((1, d), lambda bb, i: (0, 0)),
            pl.BlockSpec(wqkv_t.shape, lambda bb, i: (0, 0)),
            pl.BlockSpec(wg.shape, lambda bb, i: (0, 0)),
            pl.BlockSpec(qkw.shape, lambda bb, i: (0, 0, 0)),
            pl.BlockSpec(kb.shape, lambda bb, i: (0, 0, 0)),
            pl.BlockSpec((GQA_HEAD_DIM, ts), lambda bb, i: (0, i)),
            pl.BlockSpec((GQA_HEAD_DIM, ts), lambda bb, i: (0, i)),
        ],
        out_specs=[
            pl.BlockSpec((1, GQA_HEADS, GQA_QT_ROWS, ts), lambda bb, i: (bb, 0, 0, i)),
            pl.BlockSpec((1, GQA_KV_HEADS, ts, K_LANES), lambda bb, i: (bb, 0, i, 0)),
            pl.BlockSpec((1, GQA_KV_HEADS, ts // KV_TILE, HEAD_V, KV_TILE), lambda bb, i: (bb, 0, i, 0, 0)),
            pl.BlockSpec((1, ts, GQA_WIDTH), lambda bb, i: (bb, i, 0)),
        ],
        out_shape=[
            jax.ShapeDtypeStruct((b, GQA_HEADS, GQA_QT_ROWS, r), _BF16),
            jax.ShapeDtypeStruct((b, GQA_KV_HEADS, r, K_LANES), _BF16),
            jax.ShapeDtypeStruct((b, GQA_KV_HEADS, r // KV_TILE, HEAD_V, KV_TILE), _BF16),
            jax.ShapeDtypeStruct((b, r, GQA_WIDTH), _BF16),
        ],
        compiler_params=_compiler_params(2),
        name="gqa_prep",
    )(xr, shift, scale, nw, wqkv_t, wg, qkw, kb, cos_t, sin_t)


def _mla_prep_kernel(x_ref, *refs):
    for u in range(x_ref.shape[1] // KV_TILE):
        _mla_prep_rows(x_ref[0, u * KV_TILE:(u + 1) * KV_TILE], u, *refs)


def _mla_prep_rows(x, u, shift_ref, scale_ref, nw_ref, win_ref, kvaw_ref, qaw_ref, wkvb_ref, wqb_ref,
                   knw_ref, qw_ref, krw_ref, kb_ref, cos_ref, sin_ref, q_ref, k_ref, v_ref, g_ref):
    ts = KV_TILE
    rows = slice(u * ts, (u + 1) * ts)
    cos, sin = cos_ref[:, rows], sin_ref[:, rows]
    h = _modulated_norm(x, nw_ref[...], scale_ref[0], shift_ref[0])
    p = jnp.dot(h.astype(_BF16), win_ref[...], preferred_element_type=_F32)
    c0, c1, c2 = MLA_KV_LORA, MLA_KV_LORA + MLA_Q_LORA, MLA_KV_LORA + MLA_Q_LORA + MLA_WIDTH
    kv_a, q_a, g, kr = p[:, :c0], p[:, c0:c1], p[:, c1:c2], p[:, c2:]
    g_ref[0, rows] = _silu(g).astype(_BF16)

    def row_norm(t, w):
        return t * lax.rsqrt(jnp.mean(t * t, axis=-1, keepdims=True) + NORM_EPS) * w

    kv_an = row_norm(kv_a, kvaw_ref[...])
    q_an = row_norm(q_a, qaw_ref[...])
    kvt = jnp.dot(wkvb_ref[...], kv_an.T.astype(_BF16), preferred_element_type=_F32)
    kn = _head_rms_norm_t(kvt[:MLA_HEADS * MLA_NOPE].reshape(MLA_HEADS, MLA_NOPE, ts), knw_ref[...])
    vt = kvt[MLA_HEADS * MLA_NOPE:].reshape(MLA_HEADS, MLA_V, ts)
    ones_row = jnp.ones((MLA_HEADS, 1, ts), _F32)
    v_ref[0, :, u] = vt.astype(_BF16)

    qt = jnp.dot(wqb_ref[...], q_an.T.astype(_BF16), preferred_element_type=_F32)
    qt = _head_rms_norm_t(qt.reshape(MLA_HEADS, MLA_QK, ts), qw_ref[...])
    q_nope = qt[:, :MLA_NOPE]
    q_rope = _axial_rope_t(qt[:, MLA_NOPE:], cos, sin)
    neg_b = -(jnp.sqrt(jnp.sum(q_nope * q_nope, axis=1, keepdims=True)
                       + jnp.sum(q_rope * q_rope, axis=1, keepdims=True)) * kb_ref[...])
    q_aug = jnp.concatenate([q_nope, q_rope, _tail_rows(neg_b, MLA_QT_ROWS - MLA_QK)], axis=1)
    q_ref[0, :, :, rows] = q_aug.astype(_BF16)

    krt = kr.T[:MLA_ROPE].reshape(1, MLA_ROPE, ts)
    krt = _axial_rope_t(_head_rms_norm_t(krt, krw_ref[...]), cos, sin)
    kft = jnp.concatenate([kn, jnp.broadcast_to(krt, (MLA_HEADS, MLA_ROPE, ts)),
                           _tail_rows(ones_row, K_LANES - MLA_QK)], axis=1)
    kf = kft.reshape(MLA_HEADS * K_LANES, ts).T
    for hh in range(MLA_HEADS):
        k_ref[0, hh, rows] = kf[:, hh * K_LANES:(hh + 1) * K_LANES].astype(_BF16)


def _mla_prep_specs(b, r, ts, params):
    shift, scale, nw, win, kvaw, qaw, wkvb_t, wqb_t, knw, qw, krw, kb, cos_t, sin_t = params
    const2 = lambda bb, i: (0, 0)
    const3 = lambda bb, i: (0, 0, 0)
    in_specs = [_mod_spec(shift), _mod_spec(scale)]
    in_specs += [pl.BlockSpec(a.shape, const2) for a in (nw, win, kvaw, qaw, wkvb_t, wqb_t)]
    in_specs += [pl.BlockSpec(a.shape, const3) for a in (knw, qw, krw, kb)]
    in_specs += [pl.BlockSpec((MLA_ROPE, ts), lambda bb, i: (0, i))] * 2
    out_specs = [
        pl.BlockSpec((1, MLA_HEADS, MLA_QT_ROWS, ts), lambda bb, i: (bb, 0, 0, i)),
        pl.BlockSpec((1, MLA_HEADS, ts, K_LANES), lambda bb, i: (bb, 0, i, 0)),
        pl.BlockSpec((1, MLA_HEADS, ts // KV_TILE, HEAD_V, KV_TILE), lambda bb, i: (bb, 0, i, 0, 0)),
        pl.BlockSpec((1, ts, MLA_WIDTH), lambda bb, i: (bb, i, 0)),
    ]
    out_shape = [
        jax.ShapeDtypeStruct((b, MLA_HEADS, MLA_QT_ROWS, r), _BF16),
        jax.ShapeDtypeStruct((b, MLA_HEADS, r, K_LANES), _BF16),
        jax.ShapeDtypeStruct((b, MLA_HEADS, r // KV_TILE, HEAD_V, KV_TILE), _BF16),
        jax.ShapeDtypeStruct((b, r, MLA_WIDTH), _BF16),
    ]
    return in_specs, out_specs, out_shape


def _mla_prep(xr, *params):
    b, r, d = xr.shape
    ts = _row_tile(r)
    in_specs, out_specs, out_shape = _mla_prep_specs(b, r, ts, params)
    return pl.pallas_call(
        _mla_prep_kernel,
        grid=(b, r // ts),
        in_specs=[pl.BlockSpec((1, ts, d), lambda bb, i: (bb, i, 0))] + in_specs,
        out_specs=out_specs,
        out_shape=out_shape,
        compiler_params=_compiler_params(2),
        name="mla_prep",
    )(xr, *params)


def _attn_kernel(*refs, heads, chunks_per_head, n_lat_tiles, n_ctx_tiles):
    if n_lat_tiles:
        flag_ref, q_ref, kl_ref, vl_ref, kc_ref, vc_ref, o_ref, m_ref, l_ref, acc_ref = refs
    else:
        flag_ref, q_ref, kc_ref, vc_ref, o_ref, m_ref, l_ref, acc_ref = refs
    n_chunks = heads * chunks_per_head
    qk_rows = q_ref.shape[2]

    def group_sums(p):
        return jnp.sum(p.reshape(KV_TILE // F32_SUBLANES, F32_SUBLANES, Q_CHUNK), axis=0)

    def q_chunk(c):
        hh, cc = divmod(c, chunks_per_head)
        return hh, slice(cc * Q_CHUNK, (cc + 1) * Q_CHUNK)

    ctx_tiles = [(kc_ref, vc_ref, j) for j in range(n_ctx_tiles)]

    def scores(tile, c):
        k_ref, _, j = tile
        hh, cols = q_chunk(c)
        start = j * KV_TILE
        if not isinstance(j, int):
            start = pl.multiple_of(start, KV_TILE)
        kt = k_ref[0, 0, pl.ds(start, KV_TILE), :qk_rows]
        return jnp.dot(kt, q_ref[0, hh, :, cols], preferred_element_type=_F32)

    def pipelined(steps, consume):
        pending = [scores(tile, c) for tile, c in steps[:QK_AHEAD]]
        for t, (tile, c) in enumerate(steps):
            s = pending.pop(0)
            if t + QK_AHEAD < len(steps):
                pending.append(scores(*steps[t + QK_AHEAD]))
            consume(t, tile, c, s)

    def shift_softmax(tiles):
        part = {}

        def consume(t, tile, c, s):
            p = jnp.exp2(s)
            pv = jnp.dot(tile[1][0, 0, tile[2]], p.astype(_BF16), preferred_element_type=_F32)
            if tile is tiles[0]:
                part[c] = (pv, group_sums(p))
            else:
                part[c] = (part[c][0] + pv, part[c][1] + group_sums(p))
            if tile is tiles[-1]:
                acc_ref[c], l_ref[c] = part.pop(c)

        groups = [range(c0, c0 + CHUNKS_IN_FLIGHT) for c0 in range(0, n_chunks, CHUNKS_IN_FLIGHT)]
        pipelined([(tile, c) for group in groups for tile in tiles for c in group], consume)

    def online_softmax_update(tiles):
        def consume(t, tile, c, s):
            m_prev = m_ref[c]
            m_new = jnp.maximum(m_prev, jnp.max(s, axis=0, keepdims=True))
            alpha = jnp.exp2(m_prev - m_new)
            p = jnp.exp2(s - m_new)
            pv = jnp.dot(tile[1][0, 0, tile[2]], p.astype(_BF16), preferred_element_type=_F32)
            acc_ref[c] = alpha * acc_ref[c] + pv
            l_ref[c] = alpha * l_ref[c] + group_sums(p)
            m_ref[c] = m_new

        pipelined([(tile, c) for tile in tiles for c in range(n_chunks)], consume)

    shift_is_safe = flag_ref[0] != 0

    @pl.when(shift_is_safe)
    def _():
        lat_tiles = [(kl_ref, vl_ref, j) for j in range(n_lat_tiles)] if n_lat_tiles else []
        shift_softmax(lat_tiles + ctx_tiles)

    @pl.when(jnp.logical_not(shift_is_safe))
    def _():
        m_ref[...] = jnp.full(m_ref.shape, M_INIT, _F32)
        acc_ref[...] = jnp.zeros(acc_ref.shape, _F32)
        l_ref[...] = jnp.zeros(l_ref.shape, _F32)
        if n_lat_tiles:
            def body(jj, carry):
                online_softmax_update([(kl_ref, vl_ref, jj * ONLINE_TILES_PER_ITER + u)
                                       for u in range(ONLINE_TILES_PER_ITER)])
                return carry
            lax.fori_loop(0, n_lat_tiles // ONLINE_TILES_PER_ITER, body, 0)
        online_softmax_update(ctx_tiles)

    for c in range(n_chunks):
        hh, cols = q_chunk(c)
        denom = jnp.sum(l_ref[c], axis=0, keepdims=True)
        o_ref[0, hh, :, cols] = (acc_ref[c] / denom).astype(_BF16)


def _attention(shift_ok, qt, k_lat, v_lat, k_ctx, v_ctx, *, heads_per_step, tq, name):
    b, h, qk_rows, r = qt.shape
    tq = min(tq, r)
    kvh = k_ctx.shape[1]
    n_ctx_tiles = v_ctx.shape[2]
    n_lat_tiles = v_lat.shape[2] if v_lat is not None else 0
    chunks_per_head = tq // Q_CHUNK
    n_chunks = heads_per_step * chunks_per_head

    def kv_specs(k, v):
        return [pl.BlockSpec((1, 1) + k.shape[2:], lambda bb, g, i: (bb, g, 0, 0)),
                pl.BlockSpec((1, 1) + v.shape[2:], lambda bb, g, i: (bb, g, 0, 0, 0))]

    in_specs = [pl.BlockSpec(memory_space=pltpu.SMEM),
                pl.BlockSpec((1, heads_per_step, qk_rows, tq), lambda bb, g, i: (bb, g, 0, i))]
    args = [shift_ok, qt]
    if n_lat_tiles:
        in_specs += kv_specs(k_lat, v_lat)
        args += [k_lat, v_lat]
    in_specs += kv_specs(k_ctx, v_ctx)
    args += [k_ctx, v_ctx]
    return pl.pallas_call(
        functools.partial(_attn_kernel, heads=heads_per_step, chunks_per_head=chunks_per_head,
                          n_lat_tiles=n_lat_tiles, n_ctx_tiles=n_ctx_tiles),
        grid=(b, kvh, r // tq),
        in_specs=in_specs,
        out_specs=pl.BlockSpec((1, heads_per_step, HEAD_V, tq), lambda bb, g, i: (bb, g, 0, i)),
        out_shape=jax.ShapeDtypeStruct((b, h, HEAD_V, r), _BF16),
        scratch_shapes=[pltpu.VMEM((n_chunks, 1, Q_CHUNK), _F32),
                        pltpu.VMEM((n_chunks, F32_SUBLANES, Q_CHUNK), _F32),
                        pltpu.VMEM((n_chunks, HEAD_V, Q_CHUNK), _F32)],
        compiler_params=_compiler_params(3),
        name=name,
    )(*args)


def _outproj_rows(o_ref, g_ref, x_ref, gate_ref, w_ref, rows):
    ot = o_ref[0, :, :, rows].astype(_F32)
    o = ot.reshape(ot.shape[0] * ot.shape[1], ot.shape[2]).T
    z = (o * g_ref[0, rows].astype(_F32)).astype(_BF16)
    y = jnp.dot(z, w_ref[...], preferred_element_type=_F32)
    return x_ref[0, rows] + gate_ref[0] * y


def _outproj_kernel(o_ref, g_ref, x_ref, gate_ref, w_ref, y_ref):
    y_ref[0] = _outproj_rows(o_ref, g_ref, x_ref, gate_ref, w_ref, slice(0, x_ref.shape[1]))


def _outproj_specs(ot, xr, gate, w_out, ts):
    b, r, d = xr.shape
    heads, hv = ot.shape[1], ot.shape[2]
    in_specs = [
        pl.BlockSpec((1, heads, hv, ts), lambda bb, i: (bb, 0, 0, i)),
        pl.BlockSpec((1, ts, heads * hv), lambda bb, i: (bb, i, 0)),
        pl.BlockSpec((1, ts, d), lambda bb, i: (bb, i, 0)),
        _mod_spec(gate),
        pl.BlockSpec(w_out.shape, lambda bb, i: (0, 0)),
    ]
    return in_specs, pl.BlockSpec((1, ts, d), lambda bb, i: (bb, i, 0)), jax.ShapeDtypeStruct((b, r, d), _F32)


def _outproj(ot, gs, xr, gate, w_out):
    b, r, _ = xr.shape
    ts = _row_tile(r)
    in_specs, out_spec, out_shape = _outproj_specs(ot, xr, gate, w_out, ts)
    return pl.pallas_call(
        _outproj_kernel,
        grid=(b, r // ts),
        in_specs=in_specs,
        out_specs=out_spec,
        out_shape=out_shape,
        compiler_params=_compiler_params(2),
        name="outproj",
    )(ot, gs, xr, gate, w_out)


def _outproj_mla_prep_kernel(o_ref, g0_ref, x_ref, gate_ref, w_ref, *refs):
    prep_in, (y_ref, *prep_out) = refs[:-5], refs[-5:]
    for u in range(x_ref.shape[1] // KV_TILE):
        rows = slice(u * KV_TILE, (u + 1) * KV_TILE)
        y = _outproj_rows(o_ref, g0_ref, x_ref, gate_ref, w_ref, rows)
        y_ref[0, rows] = y
        _mla_prep_rows(y, u, *prep_in, *prep_out)


def _outproj_mla_prep(ot, gs, xr, gate, w_out, *prep_params):
    b, r, _ = xr.shape
    ts = min(FUSED_ROW_TILE, r)
    o_in, y_spec, y_shape = _outproj_specs(ot, xr, gate, w_out, ts)
    p_in, p_out_specs, p_out_shape = _mla_prep_specs(b, r, ts, prep_params)
    return pl.pallas_call(
        _outproj_mla_prep_kernel,
        grid=(b, r // ts),
        in_specs=o_in + p_in,
        out_specs=[y_spec] + p_out_specs,
        out_shape=[y_shape] + p_out_shape,
        compiler_params=_compiler_params(2),
        name="outproj_mla_prep",
    )(ot, gs, xr, gate, w_out, *prep_params)


def _axial_rope_tables_t(n_tokens, rot_dim):
    rows = n_tokens // GRID_W
    row = jnp.repeat(jnp.arange(rows, dtype=_F32), GRID_W)
    col = jnp.tile(jnp.arange(GRID_W, dtype=_F32), rows)
    axis_dim = rot_dim // 2
    inv_freq = jnp.power(ROPE_THETA, -jnp.arange(0, axis_dim, 2, dtype=_F32) / axis_dim)
    ang_r = row[:, None] * inv_freq[None, :]
    ang_c = col[:, None] * inv_freq[None, :]
    ang = jnp.concatenate([ang_r, ang_r, ang_c, ang_c], axis=-1)
    return jnp.cos(ang).T, jnp.sin(ang).T


def _identity_rope_tables_t(n_tokens, rot_dim):
    return jnp.ones((rot_dim, n_tokens), _F32), jnp.zeros((rot_dim, n_tokens), _F32)


def _split_mod(mod, n_batch):
    d = D_MODEL
    lat = tuple(mod[:n_batch, None, k * d:(k + 1) * d] for k in range(3))
    ctx = tuple(mod[n_batch:n_batch + 1, None, k * d:(k + 1) * d] for k in range(3))
    return lat, ctx


def _shift_params(q_norm_bound, k_norm_bound):
    kb = k_norm_bound * SHIFT_MARGIN
    ok = (q_norm_bound * kb <= SHIFT_LIMIT).astype(jnp.int32)
    return kb.reshape(1, 1, 1), ok.reshape(1)


def kernel(x, c, ctx, c_ctx, l0_w_mod, l0_b_mod, l0_norm, l0_w_in, l0_q_norm, l0_k_norm, l0_w_out, l1_w_mod, l1_b_mod, l1_norm, l1_w_in, l1_kv_a_norm, l1_w_kv_b, l1_q_a_norm, l1_w_q_b, l1_q_norm, l1_k_nope_norm, l1_k_rope_norm, l1_w_out):
    n_batch, seq, d = x.shape
    ctx_len = ctx.shape[1]
    cond_rows = 8
    cond = jnp.concatenate([c, c_ctx[None, :], jnp.zeros((cond_rows - n_batch - 1, d), _F32)], axis=0)

    (sh_l, sc_l, gt_l), (sh_c, sc_c, gt_c) = _split_mod(_adaln(cond, l0_w_mod, l0_b_mod), n_batch)
    kw, kv_w = GQA_KV_WIDTH, 2 * GQA_KV_WIDTH
    w_k, w_v = l0_w_in[:, :kw], l0_w_in[:, kw:kv_w]
    w_q, w_g = l0_w_in[:, kv_w:kv_w + GQA_WIDTH], l0_w_in[:, kv_w + GQA_WIDTH:]
    wqkv_t = jnp.concatenate([w_q, w_k, w_v], axis=1).T.astype(_BF16)
    q_scale = GQA_HEAD_DIM ** -0.5 * LOG2E
    qkw = jnp.concatenate([jnp.tile((l0_q_norm * q_scale)[None], (GQA_HEADS, 1)),
                           jnp.tile(l0_k_norm[None], (GQA_KV_HEADS, 1))], axis=0)[:, :, None]
    kb0, ok0 = _shift_params(GQA_HEAD_DIM ** 0.5 * q_scale * jnp.max(jnp.abs(l0_q_norm)),
                             GQA_HEAD_DIM ** 0.5 * jnp.max(jnp.abs(l0_k_norm)))
    nw0 = l0_norm[None, :]
    wg0 = w_g.astype(_BF16)
    cos_a, sin_a = _axial_rope_tables_t(seq, GQA_HEAD_DIM)
    cos_i, sin_i = _identity_rope_tables_t(ctx_len, GQA_HEAD_DIM)
    q_l, k_l, v_l, g_l = _gqa_prep(x, sh_l, sc_l, nw0, wqkv_t, wg0, qkw, kb0, cos_a, sin_a)
    q_c, k_c, v_c, g_c = _gqa_prep(ctx, sh_c, sc_c, nw0, wqkv_t, wg0, qkw, kb0, cos_i, sin_i)
    o_l = _attention(ok0, q_l, k_l, v_l, k_c, v_c, heads_per_step=GQA_GROUP, tq=1024, name="gqa_attn")
    o_c = _attention(ok0, q_c, None, None, k_c, v_c, heads_per_step=GQA_GROUP, tq=ctx_len, name="gqa_attn_ctx")
    w_out0 = l0_w_out.astype(_BF16)
    ctx = _outproj(o_c, g_c, ctx, gt_c, w_out0)

    gate0_l = gt_l
    (sh_l, sc_l, gt_l), (sh_c, sc_c, _) = _split_mod(_adaln(cond, l1_w_mod, l1_b_mod), n_batch)
    c_kv, c_kr = MLA_KV_LORA, MLA_KV_LORA + MLA_ROPE
    c_q = c_kr + MLA_Q_LORA
    win = jnp.concatenate([l1_w_in[:, :c_kv], l1_w_in[:, c_kr:c_q], l1_w_in[:, c_q:], l1_w_in[:, c_kv:c_kr],
                           jnp.zeros((d, LANES - MLA_ROPE), _F32)], axis=1).astype(_BF16)
    wkvb = l1_w_kv_b.reshape(MLA_KV_LORA, MLA_HEADS, MLA_NOPE + MLA_V)
    wkvb_t = jnp.concatenate([wkvb[:, :, :MLA_NOPE].reshape(MLA_KV_LORA, -1),
                              wkvb[:, :, MLA_NOPE:].reshape(MLA_KV_LORA, -1)], axis=1).T.astype(_BF16)
    wqb_t = l1_w_q_b.T.astype(_BF16)
    q_scale = MLA_QK ** -0.5 * LOG2E
    nw1 = l1_norm[None, :]
    kvaw, qaw = l1_kv_a_norm[None, :], l1_q_a_norm[None, :]
    knw = l1_k_nope_norm[None, :, None]
    qw = (l1_q_norm * q_scale)[None, :, None]
    krw = l1_k_rope_norm[None, :, None]
    kb1, ok1 = _shift_params(MLA_QK ** 0.5 * q_scale * jnp.max(jnp.abs(l1_q_norm)),
                             jnp.sqrt(MLA_NOPE * jnp.max(jnp.square(l1_k_nope_norm))
                                      + MLA_ROPE * jnp.max(jnp.square(l1_k_rope_norm))))
    cos_b, sin_b = _axial_rope_tables_t(seq, MLA_ROPE)
    cos_i, sin_i = _identity_rope_tables_t(ctx_len, MLA_ROPE)
    x, q_l, k_l, v_l, g_l = _outproj_mla_prep(o_l, g_l, x, gate0_l, w_out0, sh_l, sc_l, nw1, win, kvaw, qaw,
                                              wkvb_t, wqb_t, knw, qw, krw, kb1, cos_b, sin_b)
    _, k_c, v_c, _ = _mla_prep(ctx, sh_c, sc_c, nw1, win, kvaw, qaw, wkvb_t, wqb_t, knw, qw, krw, kb1,
                               cos_i, sin_i)
    o_l = _attention(ok1, q_l, k_l, v_l, k_c, v_c, heads_per_step=1, tq=4096, name="mla_attn")
    return _outproj(o_l, g_l, x, gt_l, l1_w_out.astype(_BF16))
```

```python
import functools
import math

import jax
import jax.numpy as jnp
from jax import lax
from jax.experimental import pallas as pl
from jax.experimental.pallas import tpu as pltpu

D_MODEL = 1024
GRID_W = 64
ROPE_THETA = 10000.0
NORM_EPS = 1e-6

GQA_HEADS = 16
GQA_KV_HEADS = 4
GQA_GROUP = GQA_HEADS // GQA_KV_HEADS
GQA_HEAD_DIM = 64
GQA_WIDTH = GQA_HEADS * GQA_HEAD_DIM
GQA_KV_WIDTH = GQA_KV_HEADS * GQA_HEAD_DIM

MLA_HEADS = 16
MLA_NOPE = 64
MLA_ROPE = 32
MLA_V = 64
MLA_Q_LORA = 384
MLA_KV_LORA = 256
MLA_QK = MLA_NOPE + MLA_ROPE
MLA_WIDTH = MLA_HEADS * MLA_V

LANES = 128
MXU_COLS = 256
BF16_SUBLANES = 16
VMEM_LIMIT_BYTES = 48 * 1024 * 1024

MAX_ROW_TILE = 512
FUSED_ROW_TILE = 256
KV_TILE = 256
Q_CHUNK = MXU_COLS
HEAD_V = 64
F32_SUBLANES = 8
K_LANES = LANES
GQA_QT_ROWS = GQA_HEAD_DIM + BF16_SUBLANES
MLA_QT_ROWS = K_LANES
QK_AHEAD = 4
CHUNKS_IN_FLIGHT = 2
ONLINE_TILES_PER_ITER = 4
LOG2E = math.log2(math.e)
M_INIT = -1e30
SHIFT_MARGIN = 1.0 + 2.0 ** -6
SHIFT_LIMIT = 60.0

_BF16 = jnp.bfloat16
_F32 = jnp.float32


def _silu(v):
    return v * jax.nn.sigmoid(v)


def _compiler_params(n_grid_axes):
    return pltpu.CompilerParams(
        dimension_semantics=("arbitrary",) * n_grid_axes,
        vmem_limit_bytes=VMEM_LIMIT_BYTES,
    )


def _adaln_kernel(cond_ref, w_ref, b_ref, o_ref):
    a = _silu(cond_ref[...])
    o_ref[...] = jnp.dot(a, w_ref[...], preferred_element_type=_F32,
                         precision=lax.Precision.HIGHEST) + b_ref[...]


def _adaln(cond, w_mod, b_mod):
    rows, d = cond.shape
    n = w_mod.shape[1]
    tn = 1024
    return pl.pallas_call(
        _adaln_kernel,
        grid=(n // tn,),
        in_specs=[
            pl.BlockSpec((rows, d), lambda j: (0, 0)),
            pl.BlockSpec((d, tn), lambda j: (0, j)),
            pl.BlockSpec((1, tn), lambda j: (0, j)),
        ],
        out_specs=pl.BlockSpec((rows, tn), lambda j: (0, j)),
        out_shape=jax.ShapeDtypeStruct((rows, n), _F32),
        compiler_params=_compiler_params(1),
        name="adaln",
    )(cond, w_mod, b_mod.reshape(1, n))


def _modulated_norm(x, nw, scale, shift):
    ms = jnp.mean(x * x, axis=-1, keepdims=True)
    return (x * lax.rsqrt(ms + NORM_EPS)) * (nw * (1.0 + scale)) + shift


def _head_rms_norm_t(t, w):
    ms = jnp.mean(t * t, axis=1, keepdims=True)
    return t * lax.rsqrt(ms + NORM_EPS) * w


def _axial_rope_t(t, cos, sin):
    q = t.shape[1] // 4
    x1, x2, x3, x4 = t[:, 0:q], t[:, q:2 * q], t[:, 2 * q:3 * q], t[:, 3 * q:4 * q]
    rot = jnp.concatenate([-x2, x1, -x4, x3], axis=1)
    return t * cos[None] + rot * sin[None]


def _tail_rows(first_row, n_rows):
    heads, _, cols = first_row.shape
    row = lax.broadcasted_iota(jnp.int32, (heads, n_rows, cols), 1)
    return jnp.where(row == 0, jnp.broadcast_to(first_row, (heads, n_rows, cols)), 0.0)


def _row_tile(rows):
    return min(MAX_ROW_TILE, rows)


def _shift_tail(negb, heads, n_rows, cols):
    return _tail_rows(jnp.broadcast_to(negb, (heads, 1, cols)), n_rows)


def _gqa_prep_kernel(x_ref, shift_ref, scale_ref, nw_ref, wqkv_ref, wg_ref, qkw_ref, negb_ref, cos_ref, sin_ref,
                     q_ref, k_ref, v_ref, g_ref):
    ts = KV_TILE
    for u in range(x_ref.shape[1] // ts):
        rows = slice(u * ts, (u + 1) * ts)
        h = _modulated_norm(x_ref[0, rows], nw_ref[...], scale_ref[0], shift_ref[0])
        g = jnp.dot(h.astype(_BF16), wg_ref[...], preferred_element_type=_F32)
        g_ref[0, rows] = _silu(g).astype(_BF16)
        ht = h.T.astype(_BF16)
        a = jnp.dot(wqkv_ref[...], ht, preferred_element_type=_F32)
        n_qk = GQA_HEADS + GQA_KV_HEADS
        qk = a[:n_qk * GQA_HEAD_DIM].reshape(n_qk, GQA_HEAD_DIM, ts)
        qk = _head_rms_norm_t(qk, qkw_ref[...])
        qk = _axial_rope_t(qk, cos_ref[:, rows], sin_ref[:, rows])
        q = qk[:GQA_HEADS]
        q_tail = _shift_tail(negb_ref[...], GQA_HEADS, GQA_QT_ROWS - GQA_HEAD_DIM, ts)
        q_ref[0, :, :, rows] = jnp.concatenate([q, q_tail], axis=1).astype(_BF16)
        k_tail = _tail_rows(jnp.ones((GQA_KV_HEADS, 1, ts), _F32), K_LANES - GQA_HEAD_DIM)
        kn = jnp.concatenate([qk[GQA_HEADS:], k_tail], axis=1).reshape(GQA_KV_HEADS * K_LANES, ts).T
        for hh in range(GQA_KV_HEADS):
            k_ref[0, hh, rows] = kn[:, hh * K_LANES:(hh + 1) * K_LANES].astype(_BF16)
        v_ref[0, :, u] = a[n_qk * GQA_HEAD_DIM:].reshape(GQA_KV_HEADS, GQA_HEAD_DIM, ts).astype(_BF16)


def _mod_spec(arr):
    per_batch = arr.shape[0] > 1
    return pl.BlockSpec((1, 1, D_MODEL), lambda b, i: (b if per_batch else 0, 0, 0))


def _gqa_prep(xr, shift, scale, nw, wqkv_t, wg, qkw, negb, cos_t, sin_t):
    b, r, d = xr.shape
    ts = _row_tile(r)
    return pl.pallas_call(
        _gqa_prep_kernel,
        grid=(b, r // ts),
        in_specs=[
            pl.BlockSpec((1, ts, d), lambda bb, i: (bb, i, 0)),
            _mod_spec(shift), _mod_spec(scale),
            pl.BlockSpec((1, d), lambda bb, i: (0, 0)),
            pl.BlockSpec(wqkv_t.shape, lambda bb, i: (0, 0)),
            pl.BlockSpec(wg.shape, lambda bb, i: (0, 0)),
            pl.BlockSpec(qkw.shape, lambda bb, i: (0, 0, 0)),
            pl.BlockSpec(negb.shape, lambda bb, i: (0, 0, 0)),
            pl.BlockSpec((GQA_HEAD_DIM, ts), lambda bb, i: (0, i)),
            pl.BlockSpec((GQA_HEAD_DIM, ts), lambda bb, i: (0, i)),
        ],
        out_specs=[
            pl.BlockSpec((1, GQA_HEADS, GQA_QT_ROWS, ts), lambda bb, i: (bb, 0, 0, i)),
            pl.BlockSpec((1, GQA_KV_HEADS, ts, K_LANES), lambda bb, i: (bb, 0, i, 0)),
            pl.BlockSpec((1, GQA_KV_HEADS, ts // KV_TILE, HEAD_V, KV_TILE), lambda bb, i: (bb, 0, i, 0, 0)),
            pl.BlockSpec((1, ts, GQA_WIDTH), lambda bb, i: (bb, i, 0)),
        ],
        out_shape=[
            jax.ShapeDtypeStruct((b, GQA_HEADS, GQA_QT_ROWS, r), _BF16),
            jax.ShapeDtypeStruct((b, GQA_KV_HEADS, r, K_LANES), _BF16),
            jax.ShapeDtypeStruct((b, GQA_KV_HEADS, r // KV_TILE, HEAD_V, KV_TILE), _BF16),
            jax.ShapeDtypeStruct((b, r, GQA_WIDTH), _BF16),
        ],
        compiler_params=_compiler_params(2),
        name="gqa_prep",
    )(xr, shift, scale, nw, wqkv_t, wg, qkw, negb, cos_t, sin_t)


def _mla_prep_kernel(x_ref, *refs):
    for u in range(x_ref.shape[1] // KV_TILE):
        _mla_prep_rows(x_ref[0, u * KV_TILE:(u + 1) * KV_TILE], u, *refs)


def _mla_prep_rows(x, u, shift_ref, scale_ref, nw_ref, win_ref, kvaw_ref, qaw_ref, wkvb_ref, wqb_ref,
                   knw_ref, qw_ref, krw_ref, negb_ref, cos_ref, sin_ref, q_ref, k_ref, v_ref, g_ref):
    ts = KV_TILE
    rows = slice(u * ts, (u + 1) * ts)
    cos, sin = cos_ref[:, rows], sin_ref[:, rows]
    h = _modulated_norm(x, nw_ref[...], scale_ref[0], shift_ref[0])
    p = jnp.dot(h.astype(_BF16), win_ref[...], preferred_element_type=_F32)
    c0, c1, c2 = MLA_KV_LORA, MLA_KV_LORA + MLA_Q_LORA, MLA_KV_LORA + MLA_Q_LORA + MLA_WIDTH
    kv_a, q_a, g, kr = p[:, :c0], p[:, c0:c1], p[:, c1:c2], p[:, c2:]
    g_ref[0, rows] = _silu(g).astype(_BF16)

    def row_norm(t, w):
        return t * lax.rsqrt(jnp.mean(t * t, axis=-1, keepdims=True) + NORM_EPS) * w

    kv_an = row_norm(kv_a, kvaw_ref[...])
    q_an = row_norm(q_a, qaw_ref[...])
    kvt = jnp.dot(wkvb_ref[...], kv_an.T.astype(_BF16), preferred_element_type=_F32)
    kn = _head_rms_norm_t(kvt[:MLA_HEADS * MLA_NOPE].reshape(MLA_HEADS, MLA_NOPE, ts), knw_ref[...])
    vt = kvt[MLA_HEADS * MLA_NOPE:].reshape(MLA_HEADS, MLA_V, ts)
    ones_row = jnp.ones((MLA_HEADS, 1, ts), _F32)
    v_ref[0, :, u] = vt.astype(_BF16)

    qt = jnp.dot(wqb_ref[...], q_an.T.astype(_BF16), preferred_element_type=_F32)
    qt = _head_rms_norm_t(qt.reshape(MLA_HEADS, MLA_QK, ts), qw_ref[...])
    q_nope = qt[:, :MLA_NOPE]
    q_rope = _axial_rope_t(qt[:, MLA_NOPE:], cos, sin)
    q_tail = _shift_tail(negb_ref[...], MLA_HEADS, MLA_QT_ROWS - MLA_QK, ts)
    q_aug = jnp.concatenate([q_nope, q_rope, q_tail], axis=1)
    q_ref[0, :, :, rows] = q_aug.astype(_BF16)

    krt = kr.T[:MLA_ROPE].reshape(1, MLA_ROPE, ts)
    krt = _axial_rope_t(_head_rms_norm_t(krt, krw_ref[...]), cos, sin)
    kft = jnp.concatenate([kn, jnp.broadcast_to(krt, (MLA_HEADS, MLA_ROPE, ts)),
                           _tail_rows(ones_row, K_LANES - MLA_QK)], axis=1)
    kf = kft.reshape(MLA_HEADS * K_LANES, ts).T
    for hh in range(MLA_HEADS):
        k_ref[0, hh, rows] = kf[:, hh * K_LANES:(hh + 1) * K_LANES].astype(_BF16)


def _mla_prep_specs(b, r, ts, params):
    shift, scale, nw, win, kvaw, qaw, wkvb_t, wqb_t, knw, qw, krw, negb, cos_t, sin_t = params
    const2 = lambda bb, i: (0, 0)
    const3 = lambda bb, i: (0, 0, 0)
    in_specs = [_mod_spec(shift), _mod_spec(scale)]
    in_specs += [pl.BlockSpec(a.shape, const2) for a in (nw, win, kvaw, qaw, wkvb_t, wqb_t)]
    in_specs += [pl.BlockSpec(a.shape, const3) for a in (knw, qw, krw, negb)]
    in_specs += [pl.BlockSpec((MLA_ROPE, ts), lambda bb, i: (0, i))] * 2
    out_specs = [
        pl.BlockSpec((1, MLA_HEADS, MLA_QT_ROWS, ts), lambda bb, i: (bb, 0, 0, i)),
        pl.BlockSpec((1, MLA_HEADS, ts, K_LANES), lambda bb, i: (bb, 0, i, 0)),
        pl.BlockSpec((1, MLA_HEADS, ts // KV_TILE, HEAD_V, KV_TILE), lambda bb, i: (bb, 0, i, 0, 0)),
        pl.BlockSpec((1, ts, MLA_WIDTH), lambda bb, i: (bb, i, 0)),
    ]
    out_shape = [
        jax.ShapeDtypeStruct((b, MLA_HEADS, MLA_QT_ROWS, r), _BF16),
        jax.ShapeDtypeStruct((b, MLA_HEADS, r, K_LANES), _BF16),
        jax.ShapeDtypeStruct((b, MLA_HEADS, r // KV_TILE, HEAD_V, KV_TILE), _BF16),
        jax.ShapeDtypeStruct((b, r, MLA_WIDTH), _BF16),
    ]
    return in_specs, out_specs, out_shape


def _mla_prep(xr, *params):
    b, r, d = xr.shape
    ts = _row_tile(r)
    in_specs, out_specs, out_shape = _mla_prep_specs(b, r, ts, params)
    return pl.pallas_call(
        _mla_prep_kernel,
        grid=(b, r // ts),
        in_specs=[pl.BlockSpec((1, ts, d), lambda bb, i: (bb, i, 0))] + in_specs,
        out_specs=out_specs,
        out_shape=out_shape,
        compiler_params=_compiler_params(2),
        name="mla_prep",
    )(xr, *params)


def _attn_kernel(*refs, heads, chunks_per_head, n_lat_tiles, n_ctx_tiles):
    if n_lat_tiles:
        flag_ref, q_ref, kl_ref, vl_ref, kc_ref, vc_ref, o_ref, m_ref, l_ref, acc_ref = refs
    else:
        flag_ref, q_ref, kc_ref, vc_ref, o_ref, m_ref, l_ref, acc_ref = refs
    n_chunks = heads * chunks_per_head
    qk_rows = q_ref.shape[2]

    def group_sums(p):
        return jnp.sum(p.reshape(KV_TILE // F32_SUBLANES, F32_SUBLANES, Q_CHUNK), axis=0)

    def q_chunk(c):
        hh, cc = divmod(c, chunks_per_head)
        return hh, slice(cc * Q_CHUNK, (cc + 1) * Q_CHUNK)

    ctx_tiles = [(kc_ref, vc_ref, j) for j in range(n_ctx_tiles)]

    def scores(tile, c):
        k_ref, _, j = tile
        hh, cols = q_chunk(c)
        start = j * KV_TILE
        if not isinstance(j, int):
            start = pl.multiple_of(start, KV_TILE)
        kt = k_ref[0, 0, pl.ds(start, KV_TILE), :qk_rows]
        return jnp.dot(kt, q_ref[0, hh, :, cols], preferred_element_type=_F32)

    def pipelined(steps, consume):
        pending = [scores(tile, c) for tile, c in steps[:QK_AHEAD]]
        for t, (tile, c) in enumerate(steps):
            s = pending.pop(0)
            if t + QK_AHEAD < len(steps):
                pending.append(scores(*steps[t + QK_AHEAD]))
            consume(t, tile, c, s)

    def shift_softmax(tiles):
        part = {}

        def consume(t, tile, c, s):
            p = jnp.exp2(s)
            pv = jnp.dot(tile[1][0, 0, tile[2]], p.astype(_BF16), preferred_element_type=_F32)
            if tile is tiles[0]:
                part[c] = (pv, group_sums(p))
            else:
                part[c] = (part[c][0] + pv, part[c][1] + group_sums(p))
            if tile is tiles[-1]:
                acc_ref[c], l_ref[c] = part.pop(c)

        groups = [range(c0, c0 + CHUNKS_IN_FLIGHT) for c0 in range(0, n_chunks, CHUNKS_IN_FLIGHT)]
        pipelined([(tile, c) for group in groups for tile in tiles for c in group], consume)

    def online_softmax_update(tiles):
        def consume(t, tile, c, s):
            m_prev = m_ref[c]
            m_new = jnp.maximum(m_prev, jnp.max(s, axis=0, keepdims=True))
            alpha = jnp.exp2(m_prev - m_new)
            p = jnp.exp2(s - m_new)
            pv = jnp.dot(tile[1][0, 0, tile[2]], p.astype(_BF16), preferred_element_type=_F32)
            acc_ref[c] = alpha * acc_ref[c] + pv
            l_ref[c] = alpha * l_ref[c] + group_sums(p)
            m_ref[c] = m_new

        pipelined([(tile, c) for tile in tiles for c in range(n_chunks)], consume)

    shift_is_safe = flag_ref[0] != 0

    @pl.when(shift_is_safe)
    def _():
        lat_tiles = [(kl_ref, vl_ref, j) for j in range(n_lat_tiles)] if n_lat_tiles else []
        shift_softmax(lat_tiles + ctx_tiles)

    @pl.when(jnp.logical_not(shift_is_safe))
    def _():
        m_ref[...] = jnp.full(m_ref.shape, M_INIT, _F32)
        acc_ref[...] = jnp.zeros(acc_ref.shape, _F32)
        l_ref[...] = jnp.zeros(l_ref.shape, _F32)
        if n_lat_tiles:
            def body(jj, carry):
                online_softmax_update([(kl_ref, vl_ref, jj * ONLINE_TILES_PER_ITER + u)
                                       for u in range(ONLINE_TILES_PER_ITER)])
                return carry
            lax.fori_loop(0, n_lat_tiles // ONLINE_TILES_PER_ITER, body, 0)
        online_softmax_update(ctx_tiles)

    for c in range(n_chunks):
        hh, cols = q_chunk(c)
        denom = jnp.sum(l_ref[c], axis=0, keepdims=True)
        o_ref[0, hh, :, cols] = (acc_ref[c] / denom).astype(_BF16)


def _attention(shift_ok, qt, k_lat, v_lat, k_ctx, v_ctx, *, heads_per_step, tq, name):
    b, h, qk_rows, r = qt.shape
    tq = min(tq, r)
    kvh = k_ctx.shape[1]
    n_ctx_tiles = v_ctx.shape[2]
    n_lat_tiles = v_lat.shape[2] if v_lat is not None else 0
    chunks_per_head = tq // Q_CHUNK
    n_chunks = heads_per_step * chunks_per_head

    def kv_specs(k, v):
        return [pl.BlockSpec((1, 1) + k.shape[2:], lambda bb, g, i: (bb, g, 0, 0)),
                pl.BlockSpec((1, 1) + v.shape[2:], lambda bb, g, i: (bb, g, 0, 0, 0))]

    in_specs = [pl.BlockSpec(memory_space=pltpu.SMEM),
                pl.BlockSpec((1, heads_per_step, qk_rows, tq), lambda bb, g, i: (bb, g, 0, i))]
    args = [shift_ok, qt]
    if n_lat_tiles:
        in_specs += kv_specs(k_lat, v_lat)
        args += [k_lat, v_lat]
    in_specs += kv_specs(k_ctx, v_ctx)
    args += [k_ctx, v_ctx]
    return pl.pallas_call(
        functools.partial(_attn_kernel, heads=heads_per_step, chunks_per_head=chunks_per_head,
                          n_lat_tiles=n_lat_tiles, n_ctx_tiles=n_ctx_tiles),
        grid=(b, kvh, r // tq),
        in_specs=in_specs,
        out_specs=pl.BlockSpec((1, heads_per_step, HEAD_V, tq), lambda bb, g, i: (bb, g, 0, i)),
        out_shape=jax.ShapeDtypeStruct((b, h, HEAD_V, r), _BF16),
        scratch_shapes=[pltpu.VMEM((n_chunks, 1, Q_CHUNK), _F32),
                        pltpu.VMEM((n_chunks, F32_SUBLANES, Q_CHUNK), _F32),
                        pltpu.VMEM((n_chunks, HEAD_V, Q_CHUNK), _F32)],
        compiler_params=_compiler_params(3),
        name=name,
    )(*args)


def _outproj_rows(o_ref, g_ref, x_ref, gate_ref, w_ref, rows):
    ot = o_ref[0, :, :, rows].astype(_F32)
    o = ot.reshape(ot.shape[0] * ot.shape[1], ot.shape[2]).T
    z = (o * g_ref[0, rows].astype(_F32)).astype(_BF16)
    y = jnp.dot(z, w_ref[...], preferred_element_type=_F32)
    return x_ref[0, rows] + gate_ref[0] * y


def _outproj_kernel(o_ref, g_ref, x_ref, gate_ref, w_ref, y_ref):
    y_ref[0] = _outproj_rows(o_ref, g_ref, x_ref, gate_ref, w_ref, slice(0, x_ref.shape[1]))


def _outproj_specs(ot, xr, gate, w_out, ts):
    b, r, d = xr.shape
    heads, hv = ot.shape[1], ot.shape[2]
    in_specs = [
        pl.BlockSpec((1, heads, hv, ts), lambda bb, i: (bb, 0, 0, i)),
        pl.BlockSpec((1, ts, heads * hv), lambda bb, i: (bb, i, 0)),
        pl.BlockSpec((1, ts, d), lambda bb, i: (bb, i, 0)),
        _mod_spec(gate),
        pl.BlockSpec(w_out.shape, lambda bb, i: (0, 0)),
    ]
    return in_specs, pl.BlockSpec((1, ts, d), lambda bb, i: (bb, i, 0)), jax.ShapeDtypeStruct((b, r, d), _F32)


def _outproj(ot, gs, xr, gate, w_out):
    b, r, _ = xr.shape
    ts = _row_tile(r)
    in_specs, out_spec, out_shape = _outproj_specs(ot, xr, gate, w_out, ts)
    return pl.pallas_call(
        _outproj_kernel,
        grid=(b, r // ts),
        in_specs=in_specs,
        out_specs=out_spec,
        out_shape=out_shape,
        compiler_params=_compiler_params(2),
        name="outproj",
    )(ot, gs, xr, gate, w_out)


def _outproj_mla_prep_kernel(o_ref, g0_ref, x_ref, gate_ref, w_ref, *refs):
    prep_in, (y_ref, *prep_out) = refs[:-5], refs[-5:]
    for u in range(x_ref.shape[1] // KV_TILE):
        rows = slice(u * KV_TILE, (u + 1) * KV_TILE)
        y = _outproj_rows(o_ref, g0_ref, x_ref, gate_ref, w_ref, rows)
        y_ref[0, rows] = y
        _mla_prep_rows(y, u, *prep_in, *prep_out)


def _outproj_mla_prep(ot, gs, xr, gate, w_out, *prep_params):
    b, r, _ = xr.shape
    ts = min(FUSED_ROW_TILE, r)
    o_in, y_spec, y_shape = _outproj_specs(ot, xr, gate, w_out, ts)
    p_in, p_out_specs, p_out_shape = _mla_prep_specs(b, r, ts, prep_params)
    return pl.pallas_call(
        _outproj_mla_prep_kernel,
        grid=(b, r // ts),
        in_specs=o_in + p_in,
        out_specs=[y_spec] + p_out_specs,
        out_shape=[y_shape] + p_out_shape,
        compiler_params=_compiler_params(2),
        name="outproj_mla_prep",
    )(ot, gs, xr, gate, w_out, *prep_params)


def _axial_rope_tables_t(n_tokens, rot_dim):
    rows = n_tokens // GRID_W
    row = jnp.repeat(jnp.arange(rows, dtype=_F32), GRID_W)
    col = jnp.tile(jnp.arange(GRID_W, dtype=_F32), rows)
    axis_dim = rot_dim // 2
    inv_freq = jnp.power(ROPE_THETA, -jnp.arange(0, axis_dim, 2, dtype=_F32) / axis_dim)
    ang_r = row[:, None] * inv_freq[None, :]
    ang_c = col[:, None] * inv_freq[None, :]
    ang = jnp.concatenate([ang_r, ang_r, ang_c, ang_c], axis=-1)
    return jnp.cos(ang).T, jnp.sin(ang).T


def _identity_rope_tables_t(n_tokens, rot_dim):
    return jnp.ones((rot_dim, n_tokens), _F32), jnp.zeros((rot_dim, n_tokens), _F32)


def _split_mod(mod, n_batch):
    d = D_MODEL
    lat = tuple(mod[:n_batch, None, k * d:(k + 1) * d] for k in range(3))
    ctx = tuple(mod[n_batch:n_batch + 1, None, k * d:(k + 1) * d] for k in range(3))
    return lat, ctx


def _shift_params(q_norm_bound, k_norm_bound):
    bound = q_norm_bound * k_norm_bound * SHIFT_MARGIN
    ok = (bound <= SHIFT_LIMIT).astype(jnp.int32)
    return (-bound).reshape(1, 1, 1), ok.reshape(1)


def kernel(x, c, ctx, c_ctx, l0_w_mod, l0_b_mod, l0_norm, l0_w_in, l0_q_norm, l0_k_norm, l0_w_out, l1_w_mod, l1_b_mod, l1_norm, l1_w_in, l1_kv_a_norm, l1_w_kv_b, l1_q_a_norm, l1_w_q_b, l1_q_norm, l1_k_nope_norm, l1_k_rope_norm, l1_w_out):
    n_batch, seq, d = x.shape
    ctx_len = ctx.shape[1]
    cond_rows = 8
    cond = jnp.concatenate([c, c_ctx[None, :], jnp.zeros((cond_rows - n_batch - 1, d), _F32)], axis=0)

    (sh_l, sc_l, gt_l), (sh_c, sc_c, gt_c) = _split_mod(_adaln(cond, l0_w_mod, l0_b_mod), n_batch)
    kw, kv_w = GQA_KV_WIDTH, 2 * GQA_KV_WIDTH
    w_k, w_v = l0_w_in[:, :kw], l0_w_in[:, kw:kv_w]
    w_q, w_g = l0_w_in[:, kv_w:kv_w + GQA_WIDTH], l0_w_in[:, kv_w + GQA_WIDTH:]
    wqkv_t = jnp.concatenate([w_q, w_k, w_v], axis=1).T.astype(_BF16)
    q_scale = GQA_HEAD_DIM ** -0.5 * LOG2E
    qkw = jnp.concatenate([jnp.tile((l0_q_norm * q_scale)[None], (GQA_HEADS, 1)),
                           jnp.tile(l0_k_norm[None], (GQA_KV_HEADS, 1))], axis=0)[:, :, None]
    negb0, ok0 = _shift_params(GQA_HEAD_DIM ** 0.5 * q_scale * jnp.max(jnp.abs(l0_q_norm)),
                             GQA_HEAD_DIM ** 0.5 * jnp.max(jnp.abs(l0_k_norm)))
    nw0 = l0_norm[None, :]
    wg0 = w_g.astype(_BF16)
    cos_a, sin_a = _axial_rope_tables_t(seq, GQA_HEAD_DIM)
    cos_i, sin_i = _identity_rope_tables_t(ctx_len, GQA_HEAD_DIM)
    q_l, k_l, v_l, g_l = _gqa_prep(x, sh_l, sc_l, nw0, wqkv_t, wg0, qkw, negb0, cos_a, sin_a)
    q_c, k_c, v_c, g_c = _gqa_prep(ctx, sh_c, sc_c, nw0, wqkv_t, wg0, qkw, negb0, cos_i, sin_i)
    o_l = _attention(ok0, q_l, k_l, v_l, k_c, v_c, heads_per_step=GQA_GROUP, tq=1024, name="gqa_attn")
    o_c = _attention(ok0, q_c, None, None, k_c, v_c, heads_per_step=GQA_GROUP, tq=ctx_len, name="gqa_attn_ctx")
    w_out0 = l0_w_out.astype(_BF16)
    ctx = _outproj(o_c, g_c, ctx, gt_c, w_out0)

    gate0_l = gt_l
    (sh_l, sc_l, gt_l), (sh_c, sc_c, _) = _split_mod(_adaln(cond, l1_w_mod, l1_b_mod), n_batch)
    c_kv, c_kr = MLA_KV_LORA, MLA_KV_LORA + MLA_ROPE
    c_q = c_kr + MLA_Q_LORA
    win = jnp.concatenate([l1_w_in[:, :c_kv], l1_w_in[:, c_kr:c_q], l1_w_in[:, c_q:], l1_w_in[:, c_kv:c_kr],
                           jnp.zeros((d, LANES - MLA_ROPE), _F32)], axis=1).astype(_BF16)
    wkvb = l1_w_kv_b.reshape(MLA_KV_LORA, MLA_HEADS, MLA_NOPE + MLA_V)
    wkvb_t = jnp.concatenate([wkvb[:, :, :MLA_NOPE].reshape(MLA_KV_LORA, -1),
                              wkvb[:, :, MLA_NOPE:].reshape(MLA_KV_LORA, -1)], axis=1).T.astype(_BF16)
    wqb_t = l1_w_q_b.T.astype(_BF16)
    q_scale = MLA_QK ** -0.5 * LOG2E
    nw1 = l1_norm[None, :]
    kvaw, qaw = l1_kv_a_norm[None, :], l1_q_a_norm[None, :]
    knw = l1_k_nope_norm[None, :, None]
    qw = (l1_q_norm * q_scale)[None, :, None]
    krw = l1_k_rope_norm[None, :, None]
    negb1, ok1 = _shift_params(MLA_QK ** 0.5 * q_scale * jnp.max(jnp.abs(l1_q_norm)),
                             jnp.sqrt(MLA_NOPE * jnp.max(jnp.square(l1_k_nope_norm))
                                      + MLA_ROPE * jnp.max(jnp.square(l1_k_rope_norm))))
    cos_b, sin_b = _axial_rope_tables_t(seq, MLA_ROPE)
    cos_i, sin_i = _identity_rope_tables_t(ctx_len, MLA_ROPE)
    x, q_l, k_l, v_l, g_l = _outproj_mla_prep(o_l, g_l, x, gate0_l, w_out0, sh_l, sc_l, nw1, win, kvaw, qaw,
                                              wkvb_t, wqb_t, knw, qw, krw, negb1, cos_b, sin_b)
    _, k_c, v_c, _ = _mla_prep(ctx, sh_c, sc_c, nw1, win, kvaw, qaw, wkvb_t, wqb_t, knw, qw, krw, negb1,
                               cos_i, sin_i)
    o_l = _attention(ok1, q_l, k_l, v_l, k_c, v_c, heads_per_step=1, tq=4096, name="mla_attn")
    return _outproj(o_l, g_l, x, gt_l, l1_w_out.astype(_BF16))
```
